```python
import math
import jax, jax.numpy as jnp
from jax import lax
import numpy as np

D_MODEL = 4096
BATCH = 4
SEQ = 4096
DEPTH = 4

CTX_LEN = 256
GRID_W = 64
N_MIXERS = 3
N_A_LAYERS = (DEPTH + 2) // 3
N_B_LAYERS = (DEPTH + 1) // 3
N_C_LAYERS = DEPTH // 3
ADA_RANK = 512
N_MOD = 6
EPS = 1e-6
S5_GROUP = 16
S5_GROUPS = D_MODEL // S5_GROUP
S5_STATE = 64
S5_DT_MIN = 0.001
S5_DT_MAX = 0.1
HEAD_DIM = 128
N_HEADS = D_MODEL // HEAD_DIM
N_KV_HEADS = N_HEADS // 4
ROPE_THETA = 10000.0
Q_BLOCK = 128
CONV_WIDTH = 3
D_FF = 4 * D_MODEL

kernel_name = 'hybrid_s5_gqa_shortconv_dit'


def rms_norm(x, g):
    x32 = x.astype(jnp.float32)
    y = x32 * lax.rsqrt(jnp.mean(x32 * x32, axis=-1, keepdims=True) + EPS)
    return (y * g.astype(jnp.float32)).astype(x.dtype)


def ada_modulation(cond, w_down, w_up, b):
    m = (jax.nn.silu(cond) @ w_down) @ w_up + b
    return jnp.split(m, N_MOD, axis=-1)


def modulate(h, shift, scale):
    return h * (1 + scale[:, None, :]) + shift[:, None, :]


def squared_relu_mlp(h, w1, w2):
    return jnp.square(jax.nn.relu(h @ w1)) @ w2


def _s5_discretise(a_re, a_im, log_dt, b_re, b_im):
    a_re = jnp.minimum(a_re, -1e-4)
    dt = jnp.exp(log_dt)[:, None]
    mag = jnp.exp(a_re * dt)
    ab_re = mag * jnp.cos(a_im * dt)
    ab_im = mag * jnp.sin(a_im * dt)
    den = a_re * a_re + a_im * a_im
    n_re = ab_re - 1.0
    q_re = (n_re * a_re + ab_im * a_im) / den
    q_im = (ab_im * a_re - n_re * a_im) / den
    bb_re = q_re[..., None] * b_re - q_im[..., None] * b_im
    bb_im = q_re[..., None] * b_im + q_im[..., None] * b_re
    return ab_re, ab_im, bb_re, bb_im


def _ssm_combine(e1, e2):
    a1r, a1i, b1r, b1i = e1
    a2r, a2i, b2r, b2i = e2
    return (a2r * a1r - a2i * a1i,
            a2r * a1i + a2i * a1r,
            a2r * b1r - a2i * b1i + b2r,
            a2r * b1i + a2i * b1r + b2i)


def _ssm_scan(u, ab_re, ab_im, bb_re, bb_im, init, reverse):
    n = u.shape[1]
    bu_re = jnp.einsum('bngc,gpc->nbgp', u, bb_re)
    bu_im = jnp.einsum('bngc,gpc->nbgp', u, bb_im)
    if init is not None:
        s_re, s_im = init
        pos = n - 1 if reverse else 0
        bu_re = bu_re.at[pos].add(ab_re * s_re - ab_im * s_im)
        bu_im = bu_im.at[pos].add(ab_re * s_im + ab_im * s_re)
    a_re = jnp.broadcast_to(ab_re, (n, 1) + ab_re.shape)
    a_im = jnp.broadcast_to(ab_im, (n, 1) + ab_im.shape)
    _, _, s_re, s_im = lax.associative_scan(_ssm_combine, (a_re, a_im, bu_re, bu_im),
                                            reverse=reverse, axis=0)
    return s_re, s_im


def _ssm_readout(s_re, s_im, c_re, c_im):
    return (jnp.einsum('nbgp,gcp->bngc', s_re, c_re)
            - jnp.einsum('nbgp,gcp->bngc', s_im, c_im))


def s5_mixer(hc, hx, a_re, a_im, log_dt, b_re, b_im, c_re, c_im, d_skip, w_glu, need_ctx_out):
    f32 = jnp.float32
    bsz, n_lat, _ = hx.shape
    n_ctx = hc.shape[1]
    uc = hc.astype(f32).reshape(bsz, n_ctx, S5_GROUPS, S5_GROUP)
    ux = hx.astype(f32).reshape(bsz, n_lat, S5_GROUPS, S5_GROUP)
    d = d_skip.astype(f32).reshape(S5_GROUPS, S5_GROUP)
    yx = ux * d
    yc = uc * d if need_ctx_out else None
    for direction, reverse in enumerate((False, True)):
        ab_re, ab_im, bb_re, bb_im = _s5_discretise(
            a_re[direction].astype(f32), a_im[direction].astype(f32),
            log_dt[direction].astype(f32), b_re[direction].astype(f32),
            b_im[direction].astype(f32))
        cr = c_re[direction].astype(f32)
        ci = c_im[direction].astype(f32)
        sc_re, sc_im = _ssm_scan(uc, ab_re, ab_im, bb_re, bb_im, None, reverse)
        end = 0 if reverse else n_ctx - 1
        sx_re, sx_im = _ssm_scan(ux, ab_re, ab_im, bb_re, bb_im,
                                 (sc_re[end], sc_im[end]), reverse)
        yx = yx + _ssm_readout(sx_re, sx_im, cr, ci)
        if need_ctx_out:
            yc = yc + _ssm_readout(sc_re, sc_im, cr, ci)

    def glu(y):
        z = jax.nn.gelu(y.reshape(y.shape[0], y.shape[1], D_MODEL)).astype(hx.dtype) @ w_glu
        a, g = jnp.split(z, 2, axis=-1)
        return a * jax.nn.sigmoid(g)

    return (glu(yc) if need_ctx_out else None), glu(yx)


def axial_rope_tables(n_tokens):
    rows = n_tokens // GRID_W
    row = jnp.broadcast_to(jnp.arange(rows)[:, None], (rows, GRID_W)).reshape(-1)
    col = jnp.broadcast_to(jnp.arange(GRID_W)[None, :], (rows, GRID_W)).reshape(-1)
    n_freq = HEAD_DIM // 4
    inv_freq = ROPE_THETA ** (-jnp.arange(n_freq, dtype=jnp.float32) / n_freq)
    ang_r = row.astype(jnp.float32)[:, None] * inv_freq
    ang_c = col.astype(jnp.float32)[:, None] * inv_freq
    return jnp.cos(ang_r), jnp.sin(ang_r), jnp.cos(ang_c), jnp.sin(ang_c)


def _rotate(x, cos, sin):
    x1, x2 = jnp.split(x, 2, axis=-1)
    return jnp.concatenate([x1 * cos - x2 * sin, x2 * cos + x1 * sin], axis=-1)


def apply_axial_rope(x, tables):
    cr, sr, cc, sc = [t[None, :, None, :] for t in tables]
    x32 = x.astype(jnp.float32)
    xr, xc = jnp.split(x32, 2, axis=-1)
    return jnp.concatenate([_rotate(xr, cr, sr), _rotate(xc, cc, sc)], axis=-1).astype(x.dtype)


def _attend(q, k, v):
    s = jnp.einsum('bqgrd,bsgd->bgrqs', q, k).astype(jnp.float32)
    p = jax.nn.softmax(s, axis=-1).astype(v.dtype)
    return jnp.einsum('bgrqs,bsgd->bqgrd', p, v)


def attention_mixer(hc, hx, w_qkv, q_gain, k_gain, w_o, need_ctx_out):
    rep = N_HEADS // N_KV_HEADS
    scale = HEAD_DIM ** -0.5

    def project(h):
        b, n, _ = h.shape
        q, k, v = jnp.split(h @ w_qkv, [N_HEADS * HEAD_DIM, (N_HEADS + N_KV_HEADS) * HEAD_DIM], axis=-1)
        q = rms_norm(q.reshape(b, n, N_HEADS, HEAD_DIM), q_gain)
        k = rms_norm(k.reshape(b, n, N_KV_HEADS, HEAD_DIM), k_gain)
        return q, k, v.reshape(b, n, N_KV_HEADS, HEAD_DIM)

    bsz, n_lat, _ = hx.shape
    n_ctx = hc.shape[1]
    qc, kc, vc = project(hc)
    qx, kx, vx = project(hx)
    tables = axial_rope_tables(n_lat)
    qx = apply_axial_rope(qx, tables)
    kx = apply_axial_rope(kx, tables)
    k_all = jnp.concatenate([kx, kc], axis=1)
    v_all = jnp.concatenate([vx, vc], axis=1)
    n_blk = n_lat // Q_BLOCK
    qb = (qx * scale).reshape(bsz, n_blk, Q_BLOCK, N_KV_HEADS, rep, HEAD_DIM).swapaxes(0, 1)
    ob = lax.map(lambda q: _attend(q, k_all, v_all), qb)
    ox = ob.swapaxes(0, 1).reshape(bsz, n_lat, N_HEADS * HEAD_DIM) @ w_o
    oc = None
    if need_ctx_out:
        qcs = (qc * scale).reshape(bsz, n_ctx, N_KV_HEADS, rep, HEAD_DIM)
        oc = _attend(qcs, kc, vc).reshape(bsz, n_ctx, N_HEADS * HEAD_DIM) @ w_o
    return oc, ox


def short_conv_mixer(hc, hx, w_in, conv_w, w_out, need_ctx_out):
    def run(h):
        b_gate, c_gate, v = jnp.split(h @ w_in, 3, axis=-1)
        u = c_gate * v
        u = lax.conv_general_dilated(
            u, conv_w[:, None, :].astype(u.dtype), window_strides=(1,),
            padding=((CONV_WIDTH // 2, CONV_WIDTH // 2),),
            dimension_numbers=('NWC', 'WIO', 'NWC'), feature_group_count=D_MODEL)
        return (b_gate * u) @ w_out
    return (run(hc) if need_ctx_out else None), run(hx)


def setup_inputs(seed: int = 0) -> dict:
    key = jax.random.key(seed)
    ks = iter(jax.random.split(key, 32))
    f32 = jnp.float32

    def normal(shape, std):
        return std * jax.random.normal(next(ks), shape, f32)

    G, P, Cg = S5_GROUPS, S5_STATE, S5_GROUP
    qkv_out = (N_HEADS + 2 * N_KV_HEADS) * HEAD_DIM
    return {
        'x': normal((BATCH, SEQ, D_MODEL), 1.0),
        'c': normal((BATCH, D_MODEL), 1.0),
        'ctx': normal((BATCH, CTX_LEN, D_MODEL), 1.0),
        'c_ctx': normal((D_MODEL,), 1.0),
        'ada_down': normal((DEPTH, D_MODEL, ADA_RANK), D_MODEL ** -0.5),
        'ada_up': normal((DEPTH, ADA_RANK, N_MOD * D_MODEL), 0.5 * ADA_RANK ** -0.5),
        'ada_bias': normal((DEPTH, N_MOD * D_MODEL), 0.01),
        'norm1_g': 1.0 + normal((DEPTH, D_MODEL), 0.02),
        'norm2_g': 1.0 + normal((DEPTH, D_MODEL), 0.02),
        's5_a_re': -0.5 + normal((N_A_LAYERS, 2, G, P), 0.01),
        's5_a_im': jnp.pi * jnp.arange(P, dtype=f32) + normal((N_A_LAYERS, 2, G, P), 0.01),
        's5_log_dt': jax.random.uniform(next(ks), (N_A_LAYERS, 2, G), f32,
                                        math.log(S5_DT_MIN), math.log(S5_DT_MAX)),
        's5_b_re': normal((N_A_LAYERS, 2, G, P, Cg), (2 * Cg) ** -0.5),
        's5_b_im': normal((N_A_LAYERS, 2, G, P, Cg), (2 * Cg) ** -0.5),
        's5_c_re': normal((N_A_LAYERS, 2, G, Cg, P), P ** -0.5),
        's5_c_im': normal((N_A_LAYERS, 2, G, Cg, P), P ** -0.5),
        's5_d': normal((N_A_LAYERS, D_MODEL), 1.0),
        's5_w_glu': normal((N_A_LAYERS, D_MODEL, 2 * D_MODEL), D_MODEL ** -0.5),
        'attn_w_qkv': normal((N_B_LAYERS, D_MODEL, qkv_out), D_MODEL ** -0.5),
        'attn_q_gain': 1.0 + normal((N_B_LAYERS, HEAD_DIM), 0.02),
        'attn_k_gain': 1.0 + normal((N_B_LAYERS, HEAD_DIM), 0.02),
        'attn_w_o': normal((N_B_LAYERS, N_HEADS * HEAD_DIM, D_MODEL), (N_HEADS * HEAD_DIM) ** -0.5),
        'conv_w_in': normal((N_C_LAYERS, D_MODEL, 3 * D_MODEL), D_MODEL ** -0.5),
        'conv_w': normal((N_C_LAYERS, CONV_WIDTH, D_MODEL), CONV_WIDTH ** -0.5),
        'conv_w_out': normal((N_C_LAYERS, D_MODEL, D_MODEL), D_MODEL ** -0.5),
        'mlp_w1': normal((DEPTH, D_MODEL, D_FF), D_MODEL ** -0.5),
        'mlp_w2': normal((DEPTH, D_FF, D_MODEL), D_FF ** -0.5),
    }


def reference(x, c, ctx, c_ctx, ada_down, ada_up, ada_bias, norm1_g, norm2_g,
              s5_a_re, s5_a_im, s5_log_dt, s5_b_re, s5_b_im, s5_c_re, s5_c_im, s5_d, s5_w_glu,
              attn_w_qkv, attn_q_gain, attn_k_gain, attn_w_o,
              conv_w_in, conv_w, conv_w_out, mlp_w1, mlp_w2):
    cond_ctx = c_ctx[None, :]
    for i in range(DEPTH):
        kind, j = i % N_MIXERS, i // N_MIXERS
        last = i == DEPTH - 1
        sh1, sc1, g1, sh2, sc2, g2 = ada_modulation(c, ada_down[i], ada_up[i], ada_bias[i])
        csh1, csc1, cg1, csh2, csc2, cg2 = ada_modulation(cond_ctx, ada_down[i], ada_up[i], ada_bias[i])
        hx = modulate(rms_norm(x, norm1_g[i]), sh1, sc1)
        hc = modulate(rms_norm(ctx, norm1_g[i]), csh1, csc1)
        need_ctx = not last
        if kind == 0:
            yc, yx = s5_mixer(hc, hx, s5_a_re[j], s5_a_im[j], s5_log_dt[j], s5_b_re[j], s5_b_im[j],
                              s5_c_re[j], s5_c_im[j], s5_d[j], s5_w_glu[j], need_ctx)
        elif kind == 1:
            yc, yx = attention_mixer(hc, hx, attn_w_qkv[j], attn_q_gain[j], attn_k_gain[j],
                                     attn_w_o[j], need_ctx)
        else:
            yc, yx = short_conv_mixer(hc, hx, conv_w_in[j], conv_w[j], conv_w_out[j], need_ctx)
        x = x + g1[:, None, :] * yx
        hx = modulate(rms_norm(x, norm2_g[i]), sh2, sc2)
        x = x + g2[:, None, :] * squared_relu_mlp(hx, mlp_w1[i], mlp_w2[i])
        if need_ctx:
            ctx = ctx + cg1[:, None, :] * yc
            hc = modulate(rms_norm(ctx, norm2_g[i]), csh2, csc2)
            ctx = ctx + cg2[:, None, :] * squared_relu_mlp(hc, mlp_w1[i], mlp_w2[i])
    return x
```

```python
import functools

import jax
import jax.numpy as jnp
from jax import lax
from jax.experimental import pallas as pl
from jax.experimental.pallas import tpu as pltpu

EPS = 1e-6
N_MOD = 6
S5_GROUP = 16
S5_CHUNK = 16
HEAD_DIM = 128
KV_REP = 4
GRID_W = 64
ROPE_THETA = 10000.0
LANES = 128
SUBLANES = 8
V7X_VMEM_BYTES = 64 * 1024 * 1024
VMEM_CEILING = V7X_VMEM_BYTES - 4 * 1024 * 1024
MOD_ROWS = 8

_HP = lax.Precision.HIGHEST


def _cparams(sem, vmem_bytes):
    return pltpu.CompilerParams(
        dimension_semantics=sem,
        vmem_limit_bytes=int(min(max(vmem_bytes, 16 * 1024 * 1024), VMEM_CEILING)))


def _nbytes(shape, dtype):
    n = 1
    for s in shape:
        if s is not None:
            n *= s
    return n * jnp.dtype(dtype).itemsize


def _pick(n, target, mult=LANES):
    best = None
    for d in range(mult, min(n, target) + 1, mult):
        if n % d == 0:
            best = d
    return best if best is not None else n


def _ada_kernel(cond_ref, down_ref, up_ref, b_ref, o_ref, t_ref):
    @pl.when(pl.program_id(1) == 0)
    def _():
        cnd = cond_ref[...]
        s = cnd * jax.nn.sigmoid(cnd)
        t_ref[...] = jnp.dot(s.astype(jnp.bfloat16), down_ref[...].astype(jnp.bfloat16),
                             preferred_element_type=jnp.float32)

    o_ref[...] = jnp.dot(t_ref[...].astype(jnp.bfloat16), up_ref[...].astype(jnp.bfloat16),
                         preferred_element_type=jnp.float32) + b_ref[...]


def _ada_all(cond, ada_down, ada_up, ada_bias):
    depth, d, r = ada_down.shape
    n = ada_up.shape[2]
    bn = _pick(n, 2048)
    vmem = 2 * (_nbytes((MOD_ROWS, d), jnp.float32) + _nbytes((d, r), jnp.float32)
                + _nbytes((r, bn), jnp.float32) + 2 * _nbytes((MOD_ROWS, bn), jnp.float32))
    return pl.pallas_call(
        _ada_kernel,
        grid=(depth, n // bn),
        in_specs=[
            pl.BlockSpec((MOD_ROWS, d), lambda l, j: (0, 0)),
            pl.BlockSpec((None, d, r), lambda l, j: (l, 0, 0)),
            pl.BlockSpec((None, r, bn), lambda l, j: (l, 0, j)),
            pl.BlockSpec((None, 1, bn), lambda l, j: (l, 0, j)),
        ],
        out_specs=pl.BlockSpec((None, MOD_ROWS, bn), lambda l, j: (l, 0, j)),
        out_shape=jax.ShapeDtypeStruct((depth, MOD_ROWS, n), jnp.float32),
        scratch_shapes=[pltpu.VMEM((MOD_ROWS, r), jnp.float32)],
        compiler_params=_cparams(("arbitrary", "arbitrary"), 2 * vmem),
        name="ada_mod",
    )(cond, ada_down, ada_up, ada_bias.reshape(depth, 1, n))


def _norm_kernel(x_ref, g_ref, sh_ref, sc_ref, o_ref):
    x = x_ref[...]
    ms = jnp.mean(x * x, axis=-1, keepdims=True)
    y = x * lax.rsqrt(ms + EPS) * g_ref[...]
    o_ref[...] = (y * (1.0 + sc_ref[...]) + sh_ref[...]).astype(o_ref.dtype)


class _Layout:
    def __init__(self, batch, seq, ctx, d):
        self.batch, self.seq, self.ctx, self.d = batch, seq, ctx, d
        self.tl, self.tc = batch * seq, batch * ctx
        self.t = self.tl + self.tc
        self.bm = _pick(self.tc, 1024, SUBLANES)
        assert seq % self.bm == 0 and self.tc % self.bm == 0
        assert batch + 1 <= MOD_ROWS

    def mod_row(self, i, bm):
        return jnp.where(i * bm < self.tl, (i * bm) // self.seq, self.batch)


def _mod_spec(lay, which, bm, bn, col=True):
    if col:
        return pl.BlockSpec((None, None, 1, bn),
                            lambda i, j, *_: (which, lay.mod_row(i, bm), 0, j))
    return pl.BlockSpec((None, None, 1, bn),
                        lambda i, *_: (which, lay.mod_row(i, bm), 0, 0))


def _norm_mod(lay, x, g, mod, shift_idx, scale_idx):
    t, d = x.shape
    bm = _pick(lay.bm, 256, SUBLANES)
    vmem = 2 * (_nbytes((bm, d), jnp.float32) + _nbytes((bm, d), jnp.bfloat16)) \
        + 4 * _nbytes((bm, d), jnp.float32)
    return pl.pallas_call(
        _norm_kernel,
        grid=(t // bm,),
        in_specs=[
            pl.BlockSpec((bm, d), lambda i: (i, 0)),
            pl.BlockSpec((1, d), lambda i: (0, 0)),
            _mod_spec(lay, shift_idx, bm, d, col=False),
            _mod_spec(lay, scale_idx, bm, d, col=False),
        ],
        out_specs=pl.BlockSpec((bm, d), lambda i: (i, 0)),
        out_shape=jax.ShapeDtypeStruct((t, d), jnp.bfloat16),
        compiler_params=_cparams(("parallel",), vmem),
        name="norm_mod",
    )(x, g.reshape(1, d), mod, mod)


def _mm_kernel(*refs, n_rhs, n_extra, n_out, nk, epilogue):
    lhs_ref = refs[0]
    rhs_refs = refs[1:1 + n_rhs]
    extra_refs = refs[1 + n_rhs:1 + n_rhs + n_extra]
    out_refs = refs[1 + n_rhs + n_extra:1 + n_rhs + n_extra + n_out]
    acc_refs = refs[1 + n_rhs + n_extra + n_out:]
    i, j = pl.program_id(0), pl.program_id(1)
    if nk == 1:
        accs = [jnp.dot(lhs_ref[...], r[...], preferred_element_type=jnp.float32)
                for r in rhs_refs]
        epilogue(accs, extra_refs, out_refs, i, j)
        return
    k = pl.program_id(2)

    @pl.when(k == 0)
    def _():
        for a in acc_refs:
            a[...] = jnp.zeros_like(a)

    for a, r in zip(acc_refs, rhs_refs):
        a[...] += jnp.dot(lhs_ref[...], r[...], preferred_element_type=jnp.float32)

    @pl.when(k == nk - 1)
    def _():
        epilogue([a[...] for a in acc_refs], extra_refs, out_refs, i, j)


def _matmul(name, lhs, rhs_list, extras, outs, epilogue, *, bm, bn, nn, bk=None,
            epilogue_bytes=0):
    m, kdim = lhs.shape
    bk = kdim if bk is None else bk
    nk = kdim // bk
    assert m % bm == 0 and kdim % bk == 0
    in_specs = [pl.BlockSpec((bm, bk), lambda i, j, k: (i, k))]
    for _, off in rhs_list:
        in_specs.append(pl.BlockSpec((bk, bn), lambda i, j, k, off=off: (k, off + j)))
    in_specs += [s for _, s in extras]
    scratch = [] if nk == 1 else [pltpu.VMEM((bm, bn), jnp.float32) for _ in rhs_list]
    vmem = 2 * _nbytes((bm, bk), lhs.dtype)
    vmem += sum(2 * _nbytes((bk, bn), w.dtype) for w, _ in rhs_list)
    vmem += sum(2 * _nbytes(s.block_shape, a.dtype) for a, s in extras)
    vmem += sum(2 * _nbytes(s.block_shape, a.dtype) for a, s in outs)
    vmem += 2 * len(rhs_list) * _nbytes((bm, bn), jnp.float32) + epilogue_bytes
    kern = functools.partial(_mm_kernel, n_rhs=len(rhs_list), n_extra=len(extras),
                             n_out=len(outs), nk=nk, epilogue=epilogue)
    res = pl.pallas_call(
        kern,
        grid=(m // bm, nn, nk),
        in_specs=in_specs,
        out_specs=[s for _, s in outs],
        out_shape=[a for a, _ in outs],
        scratch_shapes=scratch,
        compiler_params=_cparams(("parallel", "parallel", "arbitrary"), vmem),
        name=name,
    )(lhs, *[w for w, _ in rhs_list], *[a for a, _ in extras])
    return res


def _tile_spec(bm, bn):
    return pl.BlockSpec((bm, bn), lambda i, j, k: (i, j))


def _residual_epilogue(accs, extra_refs, out_refs, i, j):
    res_ref, gate_ref = extra_refs
    out_refs[0][...] = res_ref[...] + gate_ref[...] * accs[0]


def _glu_residual_epilogue(accs, extra_refs, out_refs, i, j):
    res_ref, gate_ref = extra_refs
    a, g = accs
    out_refs[0][...] = res_ref[...] + gate_ref[...] * (a * jax.nn.sigmoid(g))


def _relu2_epilogue(accs, extra_refs, out_refs, i, j):
    r = jnp.maximum(accs[0], 0.0)
    out_refs[0][...] = (r * r).astype(out_refs[0].dtype)


def _gated_residual_matmul(name, lay, lhs, w, x_res, mod, gate_idx, *, glu=False, bk=None):
    t, d = x_res.shape
    bm = lay.bm
    bn = _pick(d, 512 if glu else 1024)
    nn = d // bn
    rhs = [(w, 0), (w, nn)] if glu else [(w, 0)]
    extras = [(x_res, _tile_spec(bm, bn)), (mod, _mod_spec(lay, gate_idx, bm, bn))]
    outs = [(jax.ShapeDtypeStruct((t, d), jnp.float32), _tile_spec(bm, bn))]
    epi = _glu_residual_epilogue if glu else _residual_epilogue
    return _matmul(name, lhs, rhs, extras, outs, epi, bm=bm, bn=bn, nn=nn, bk=bk,
                   epilogue_bytes=2 * _nbytes((bm, bn), jnp.float32))[0]


def _mlp(lay, h, w1, w2, x_res, mod):
    t, d = x_res.shape
    dff = w1.shape[1]
    bm = lay.bm
    bn = _pick(dff, 1024)
    a = _matmul("mlp_up", h, [(w1, 0)], [],
                [(jax.ShapeDtypeStruct((t, dff), jnp.bfloat16), _tile_spec(bm, bn))],
                _relu2_epilogue, bm=bm, bn=bn, nn=dff // bn,
                epilogue_bytes=_nbytes((bm, bn), jnp.float32))[0]
    return _gated_residual_matmul("mlp_down", lay, a, w2, x_res, mod, 5,
                                  bk=_pick(dff, 2048))


def _cmul(ar, ai, br, bi):
    return ar * br - ai * bi, ar * bi + ai * br


def _s5_prepare(a_re, a_im, log_dt, b_re, b_im, c_re, c_im, d_skip):
    f32 = jnp.float32
    L = S5_CHUNK
    a_re = jnp.minimum(a_re.astype(f32), -1e-4)
    a_im = a_im.astype(f32)
    dt = jnp.exp(log_dt.astype(f32))[..., None]
    mag = jnp.exp(a_re * dt)
    ab_re = mag * jnp.cos(a_im * dt)
    ab_im = mag * jnp.sin(a_im * dt)
    den = a_re * a_re + a_im * a_im
    n_re = ab_re - 1.0
    q_re = (n_re * a_re + ab_im * a_im) / den
    q_im = (ab_im * a_re - n_re * a_im) / den
    b_re, b_im = b_re.astype(f32), b_im.astype(f32)
    bb_re = q_re[..., None] * b_re - q_im[..., None] * b_im
    bb_im = q_re[..., None] * b_im + q_im[..., None] * b_re
    c_re, c_im = c_re.astype(f32), c_im.astype(f32)
    bb_re, bb_im, c_re, c_im = [v.astype(jnp.bfloat16).astype(f32) for v in (bb_re, bb_im, c_re, c_im)]
    ngroups, nstate = ab_re.shape[1], ab_re.shape[2]
    cg = b_re.shape[-1]

    pr, pi = [jnp.ones_like(ab_re)], [jnp.zeros_like(ab_im)]
    for _ in range(L):
        nr, ni = _cmul(pr[-1], pi[-1], ab_re, ab_im)
        pr.append(nr)
        pi.append(ni)
    pw_re, pw_im = jnp.stack(pr), jnp.stack(pi)

    cp_re, cp_im = _cmul(c_re[None], c_im[None], pw_re[:L, :, :, None, :], pw_im[:L, :, :, None, :])
    kern = (jnp.einsum('kdgcp,dgpe->kdgce', cp_re, bb_re, precision=_HP)
            - jnp.einsum('kdgcp,dgpe->kdgce', cp_im, bb_im, precision=_HP))
    kf, kb = kern[:, 0], kern[:, 1]
    diag = jnp.eye(cg, dtype=f32)[None] * d_skip.astype(f32).reshape(ngroups, cg)[:, :, None]
    k_all = jnp.concatenate([kb[:0:-1], (kf[0] + kb[0] + diag)[None], kf[1:]], axis=0)
    s_idx = jnp.arange(L)[:, None]
    t_idx = jnp.arange(L)[None, :]
    tmat = k_all[t_idx - s_idx + (L - 1)]
    tmat = tmat.transpose(2, 0, 4, 1, 3).reshape(ngroups, L * cg, L * cg)

    def win_dir(d, pows_re, pows_im):
        wr, wi = _cmul(pows_re[:, :, None, :], pows_im[:, :, None, :],
                       jnp.swapaxes(bb_re[d], 1, 2)[None], jnp.swapaxes(bb_im[d], 1, 2)[None])
        return wr.transpose(1, 0, 2, 3), wi.transpose(1, 0, 2, 3)

    wf_re, wf_im = win_dir(0, pw_re[L - 1::-1, 0][:L], pw_im[L - 1::-1, 0][:L])
    wb_re, wb_im = win_dir(1, pw_re[:L, 1], pw_im[:L, 1])
    w4 = jnp.stack([wf_re, wf_im, wb_re, wb_im], axis=3)
    w4 = w4.reshape(ngroups // 2, 2, L * cg, 4, nstate)
    eye2 = jnp.eye(2, dtype=f32)
    win = jnp.einsum('gqrap,qw->gqrawp', w4, eye2).reshape(ngroups // 2, 2 * L * cg, 8 * nstate)

    def wout_dir(d, pows_re, pows_im):
        vr, vi = _cmul(c_re[d][None], c_im[d][None], pows_re[:, :, None, :], pows_im[:, :, None, :])
        return vr.transpose(1, 3, 0, 2), vi.transpose(1, 3, 0, 2)

    vf_re, vf_im = wout_dir(0, pw_re[1:L + 1, 0], pw_im[1:L + 1, 0])
    vb_re, vb_im = wout_dir(1, pw_re[L:0:-1, 1], pw_im[L:0:-1, 1])
    v4 = jnp.stack([vf_re, -vf_im, vb_re, -vb_im], axis=1)
    v4 = v4.reshape(ngroups // 2, 2, 4, nstate, L * cg)
    wout = jnp.einsum('gqapr,qw->gawpqr', v4, eye2).reshape(ngroups // 2, 8 * nstate, 2 * L * cg)

    a4 = jnp.stack([pw_re[L, 0], pw_im[L, 0], pw_re[L, 1], pw_im[L, 1]], axis=0)
    a4 = a4.reshape(4, ngroups // 2, 2 * nstate).transpose(1, 0, 2)
    a8 = jnp.concatenate([a4, jnp.zeros_like(a4)], axis=1)
    bf16 = jnp.bfloat16
    return (tmat.reshape(ngroups // 2, 2, L * cg, L * cg).astype(bf16), win.astype(bf16),
            wout.astype(bf16), a8)


def _s5_kernel(u_ref, t_ref, win_ref, wout_ref, a_ref, y_ref, z_ref, s_ref, *, nb, jl, jc):
    half = u_ref.shape[1] // 2
    sw = z_ref.shape[1] // 4
    z_ref[...] = jnp.dot(u_ref[...], win_ref[...], preferred_element_type=jnp.float32)
    a = a_ref[...]
    afr, afi, abr, abi = [jnp.broadcast_to(a[r:r + 1, :], (nb, sw)) for r in range(4)]
    cps = max(1, SUBLANES // nb)
    step_rows = cps * nb

    def sweep(n, base, carry):
        def body(i, carry):
            sfr, sfi, sbr, sbi = carry
            rf = pl.multiple_of(base + i * step_rows, SUBLANES)
            rb = pl.multiple_of(base + (n // cps - 1 - i) * step_rows, SUBLANES)
            zfr = z_ref[pl.ds(rf, step_rows), 0 * sw:1 * sw]
            zfi = z_ref[pl.ds(rf, step_rows), 1 * sw:2 * sw]
            zbr = z_ref[pl.ds(rb, step_rows), 2 * sw:3 * sw]
            zbi = z_ref[pl.ds(rb, step_rows), 3 * sw:4 * sw]
            pfr, pfi, pbr, pbi = [], [], [], []
            for c in range(cps):
                lo = slice(c * nb, (c + 1) * nb)
                hi = slice((cps - 1 - c) * nb, (cps - c) * nb)
                pfr.append(sfr)
                pfi.append(sfi)
                pbr.insert(0, sbr)
                pbi.insert(0, sbi)
                sfr, sfi = (afr * sfr - afi * sfi + zfr[lo], afr * sfi + afi * sfr + zfi[lo])
                sbr, sbi = (abr * sbr - abi * sbi + zbr[hi], abr * sbi + abi * sbr + zbi[hi])
            s_ref[pl.ds(rf, step_rows), 0 * sw:1 * sw] = jnp.concatenate(pfr, axis=0)
            s_ref[pl.ds(rf, step_rows), 1 * sw:2 * sw] = jnp.concatenate(pfi, axis=0)
            s_ref[pl.ds(rb, step_rows), 2 * sw:3 * sw] = jnp.concatenate(pbr, axis=0)
            s_ref[pl.ds(rb, step_rows), 3 * sw:4 * sw] = jnp.concatenate(pbi, axis=0)
            return sfr, sfi, sbr, sbi

        return lax.fori_loop(0, n // cps, body, carry)

    zero = jnp.zeros((nb, sw), jnp.float32)
    carry = sweep(jc, jl * nb, (zero, zero, zero, zero))
    sweep(jl, 0, carry)

    carried = jnp.dot(s_ref[...].astype(jnp.bfloat16), wout_ref[...],
                      preferred_element_type=jnp.float32)
    for q in range(2):
        yq = jnp.dot(u_ref[:, q * half:(q + 1) * half], t_ref[q],
                     preferred_element_type=jnp.float32)
        yq = yq + carried[:, q * half:(q + 1) * half]
        y_ref[:, q * half:(q + 1) * half] = jax.nn.gelu(yq).astype(y_ref.dtype)


def _s5_mixer(lay, h, params):
    tmat, win, wout, a8 = params
    b, d = lay.batch, lay.d
    L = S5_CHUNK
    g2 = d // (2 * S5_GROUP)
    jl, jc = lay.seq // L, lay.ctx // L
    cps = max(1, SUBLANES // b)
    assert (cps * b) % SUBLANES == 0 and jl % cps == 0 and jc % cps == 0
    cw = 2 * L * S5_GROUP

    def to_chunks(part, j):
        p = part.reshape(b, j, L, g2, 2, S5_GROUP)
        return p.transpose(3, 1, 0, 4, 2, 5).reshape(g2, j * b, cw)

    u = jnp.concatenate([to_chunks(h[:lay.tl], jl), to_chunks(h[lay.tl:], jc)], axis=1)
    rows = (jl + jc) * b
    vmem = 4 * _nbytes((rows, cw), jnp.bfloat16) + 2 * 3 * _nbytes((cw, cw), jnp.bfloat16) \
        + 6 * _nbytes((rows, cw), jnp.float32)
    y = pl.pallas_call(
        functools.partial(_s5_kernel, nb=b, jl=jl, jc=jc),
        grid=(g2,),
        in_specs=[
            pl.BlockSpec((None, rows, cw), lambda g: (g, 0, 0)),
            pl.BlockSpec((None, 2, cw // 2, cw // 2), lambda g: (g, 0, 0, 0)),
            pl.BlockSpec((None, cw, cw), lambda g: (g, 0, 0)),
            pl.BlockSpec((None, cw, cw), lambda g: (g, 0, 0)),
            pl.BlockSpec((None, SUBLANES, a8.shape[2]), lambda g: (g, 0, 0)),
        ],
        out_specs=pl.BlockSpec((None, rows, cw), lambda g: (g, 0, 0)),
        out_shape=jax.ShapeDtypeStruct((g2, rows, cw), jnp.bfloat16),
        scratch_shapes=[pltpu.VMEM((rows, cw), jnp.float32), pltpu.VMEM((rows, cw), jnp.float32)],
        compiler_params=_cparams(("parallel",), vmem),
        name="s5_chunked",
    )(u, tmat, win, wout, a8)

    def from_chunks(part, j):
        p = part.reshape(g2, j, b, 2, L, S5_GROUP)
        return p.transpose(2, 1, 4, 0, 3, 5).reshape(b * j * L, d)

    return jnp.concatenate([from_chunks(y[:, :jl * b], jl), from_chunks(y[:, jl * b:], jc)], axis=0)


def _rope_tables(lay, bm):
    n = lay.seq
    rows = n // GRID_W
    row = jnp.broadcast_to(jnp.arange(rows)[:, None], (rows, GRID_W)).reshape(-1)
    col = jnp.broadcast_to(jnp.arange(GRID_W)[None, :], (rows, GRID_W)).reshape(-1)
    n_freq = HEAD_DIM // 4
    inv_freq = ROPE_THETA ** (-jnp.arange(n_freq, dtype=jnp.float32) / n_freq)
    ang_r = row.astype(jnp.float32)[:, None] * inv_freq
    ang_c = col.astype(jnp.float32)[:, None] * inv_freq
    cr, sr, cc, sc = jnp.cos(ang_r), jnp.sin(ang_r), jnp.cos(ang_c), jnp.sin(ang_c)
    cos = jnp.concatenate([cr, cr, cc, cc], axis=-1)
    sin = jnp.concatenate([-sr, sr, -sc, sc], axis=-1)
    cos = jnp.concatenate([cos, jnp.ones((bm, HEAD_DIM), jnp.float32)], axis=0)
    sin = jnp.concatenate([sin, jnp.zeros((bm, HEAD_DIM), jnp.float32)], axis=0)
    return cos, sin


def _norm_rope(x, gain, cos, sin, scale):
    ms = jnp.mean(x * x, axis=-1, keepdims=True)
    xn = x * lax.rsqrt(ms + EPS) * gain
    q4 = HEAD_DIM // 4
    lane = lax.broadcasted_iota(jnp.int32, xn.shape, 1)
    partner = jnp.where(lane % (2 * q4) < q4,
                        pltpu.roll(xn, HEAD_DIM - q4, 1),
                        pltpu.roll(xn, q4, 1))
    out = xn * cos + partner * sin
    return out * scale if scale != 1.0 else out


def _q_epilogue(accs, extra_refs, out_refs, i, j, *, scale):
    gain_ref, cos_ref, sin_ref = extra_refs
    acc = accs[0]
    cos, sin, gain = cos_ref[...], sin_ref[...], gain_ref[...]
    for hh in range(acc.shape[1] // HEAD_DIM):
        sl = slice(hh * HEAD_DIM, (hh + 1) * HEAD_DIM)
        out_refs[0][:, sl] = _norm_rope(acc[:, sl], gain, cos, sin, scale).astype(out_refs[0].dtype)


def _kv_epilogue(accs, extra_refs, out_refs, i, j):
    gain_ref, cos_ref, sin_ref = extra_refs
    k_acc, v_acc = accs
    cos, sin, gain = cos_ref[...], sin_ref[...], gain_ref[...]
    for hh in range(k_acc.shape[1] // HEAD_DIM):
        sl = slice(hh * HEAD_DIM, (hh + 1) * HEAD_DIM)
        out_refs[0][:, sl] = _norm_rope(k_acc[:, sl], gain, cos, sin, 1.0).astype(out_refs[0].dtype)
    out_refs[1][...] = v_acc.astype(out_refs[1].dtype)


def _attn_kernel(q_ref, kx_ref, kc_ref, vx_ref, vc_ref, o_ref, *, with_latent):
    dn = (((1,), (1,)), ((), ()))
    for r in range(KV_REP):
        sl = slice(r * HEAD_DIM, (r + 1) * HEAD_DIM)
        q = q_ref[:, sl]
        sc = lax.dot_general(q, kc_ref[...], dn, preferred_element_type=jnp.float32)
        m = jnp.max(sc, axis=-1, keepdims=True)
        if with_latent:
            sx = lax.dot_general(q, kx_ref[...], dn, preferred_element_type=jnp.float32)
            m = jnp.maximum(m, jnp.max(sx, axis=-1, keepdims=True))
            px = jnp.exp(sx - m)
        pc = jnp.exp(sc - m)
        l = jnp.sum(pc, axis=-1, keepdims=True)
        o = jnp.dot(pc.astype(vc_ref.dtype), vc_ref[...], preferred_element_type=jnp.float32)
        if with_latent:
            l = l + jnp.sum(px, axis=-1, keepdims=True)
            o = o + jnp.dot(px.astype(vx_ref.dtype), vx_ref[...], preferred_element_type=jnp.float32)
        o_ref[:, sl] = (o / l).astype(o_ref.dtype)


def _attention(lay, q, k, v):
    t, dq = q.shape
    nkv = k.shape[1] // HEAD_DIM
    b, seq, ctx = lay.batch, lay.seq, lay.ctx
    qw = KV_REP * HEAD_DIM
    tq = _pick(seq, 256, SUBLANES)
    nq = seq // tq
    cblk0 = lay.tl // ctx
    kx_spec = pl.BlockSpec((seq, HEAD_DIM), lambda bi, g, qi: (bi, g))
    kc_spec = pl.BlockSpec((ctx, HEAD_DIM), lambda bi, g, qi: (cblk0 + bi, g))
    vmem = 4 * _nbytes((tq, qw), jnp.bfloat16) + 8 * _nbytes((seq + ctx, HEAD_DIM), jnp.bfloat16) \
        + 4 * _nbytes((tq, seq + ctx), jnp.float32)
    o_lat = pl.pallas_call(
        functools.partial(_attn_kernel, with_latent=True),
        grid=(b, nkv, nq),
        in_specs=[pl.BlockSpec((tq, qw), lambda bi, g, qi: (bi * nq + qi, g)),
                  kx_spec, kc_spec, kx_spec, kc_spec],
        out_specs=pl.BlockSpec((tq, qw), lambda bi, g, qi: (bi * nq + qi, g)),
        out_shape=jax.ShapeDtypeStruct((t, dq), jnp.bfloat16),
        compiler_params=_cparams(("parallel", "parallel", "arbitrary"), vmem),
        name="attn_latent",
    )(q, k, k, v, v)
    kc2 = pl.BlockSpec((ctx, HEAD_DIM), lambda bi, g: (cblk0 + bi, g))

    def ctx_kernel(q_ref, kc_ref, vc_ref, prev_ref, o_ref):
        del prev_ref
        _attn_kernel(q_ref, None, kc_ref, None, vc_ref, o_ref, with_latent=False)

    return pl.pallas_call(
        ctx_kernel,
        grid=(b, nkv),
        in_specs=[pl.BlockSpec((ctx, qw), lambda bi, g: (cblk0 + bi, g)), kc2, kc2,
                  pl.BlockSpec(memory_space=pl.ANY)],
        out_specs=pl.BlockSpec((ctx, qw), lambda bi, g: (cblk0 + bi, g)),
        out_shape=jax.ShapeDtypeStruct((t, dq), jnp.bfloat16),
        input_output_aliases={3: 0},
        compiler_params=_cparams(("parallel", "parallel"), 16 * 1024 * 1024),
        name="attn_context",
    )(q, k, v, o_lat)


def _attention_mixer(lay, h, wq, wk, wv, q_gain, k_gain):
    t, d = h.shape
    bm = lay.bm
    dq, dkv = wq.shape[1], wk.shape[1]
    cos, sin = _rope_tables(lay, bm)
    nseq = lay.seq // bm

    def pos_block(i):
        return jnp.where(i * bm < lay.tl, i % nseq, nseq)

    tab_spec = pl.BlockSpec((bm, HEAD_DIM), lambda i, j, k: (pos_block(i), 0))
    gain_spec = pl.BlockSpec((1, HEAD_DIM), lambda i, j, k: (0, 0))
    bn = _pick(dq, 1024)
    q = _matmul("attn_q", h, [(wq, 0)],
                [(q_gain.reshape(1, HEAD_DIM), gain_spec), (cos, tab_spec), (sin, tab_spec)],
                [(jax.ShapeDtypeStruct((t, dq), jnp.bfloat16), _tile_spec(bm, bn))],
                functools.partial(_q_epilogue, scale=HEAD_DIM ** -0.5),
                bm=bm, bn=bn, nn=dq // bn, epilogue_bytes=4 * _nbytes((bm, bn), jnp.float32))[0]
    bn = _pick(dkv, 512)
    k, v = _matmul("attn_kv", h, [(wk, 0), (wv, 0)],
                   [(k_gain.reshape(1, HEAD_DIM), gain_spec), (cos, tab_spec), (sin, tab_spec)],
                   [(jax.ShapeDtypeStruct((t, dkv), jnp.bfloat16), _tile_spec(bm, bn)),
                    (jax.ShapeDtypeStruct((t, dkv), jnp.bfloat16), _tile_spec(bm, bn))],
                   _kv_epilogue, bm=bm, bn=bn, nn=dkv // bn,
                   epilogue_bytes=4 * _nbytes((bm, bn), jnp.float32))
    return _attention(lay, q, k, v)


def _conv_in_epilogue(accs, extra_refs, out_refs, i, j):
    b_gate, c_gate, v = accs
    out_refs[0][...] = b_gate.astype(out_refs[0].dtype)
    out_refs[1][...] = (c_gate * v).astype(out_refs[1].dtype)


def _conv_kernel(b_ref, u_ref, up_ref, un_ref, w_ref, o_ref, *, rb, lay):
    i = pl.program_id(0)
    row0 = i * rb
    is_lat = row0 < lay.tl
    seq_len = jnp.where(is_lat, lay.seq, lay.ctx)
    local = jnp.where(is_lat, row0, row0 - lay.tl) % seq_len
    first = local == 0
    last = local + rb == seq_len
    u = u_ref[...].astype(jnp.float32)
    prev_row = jnp.where(first, 0.0, up_ref[SUBLANES - 1:SUBLANES, :].astype(jnp.float32))
    next_row = jnp.where(last, 0.0, un_ref[0:1, :].astype(jnp.float32))
    ridx = lax.broadcasted_iota(jnp.int32, u.shape, 0)
    u_prev = jnp.where(ridx == 0, prev_row, pltpu.roll(u, 1, 0))
    u_next = jnp.where(ridx == rb - 1, next_row, pltpu.roll(u, rb - 1, 0))
    w = w_ref[...]
    conv = w[0:1, :] * u_prev + w[1:2, :] * u + w[2:3, :] * u_next
    o_ref[...] = (b_ref[...].astype(jnp.float32) * conv).astype(o_ref.dtype)


def _short_conv(lay, b_gate, u, conv_w):
    t, d = u.shape
    rb = _pick(lay.ctx, 256, SUBLANES)
    assert lay.ctx % rb == 0 and lay.seq % rb == 0
    bc = _pick(d, 1024)
    sub = rb // SUBLANES
    nsub = t // SUBLANES
    w8 = jnp.concatenate([conv_w.astype(jnp.float32),
                          jnp.zeros((SUBLANES - conv_w.shape[0], d), jnp.float32)], axis=0)
    return pl.pallas_call(
        functools.partial(_conv_kernel, rb=rb, lay=lay),
        grid=(t // rb, d // bc),
        in_specs=[
            pl.BlockSpec((rb, bc), lambda i, j: (i, j)),
            pl.BlockSpec((rb, bc), lambda i, j: (i, j)),
            pl.BlockSpec((SUBLANES, bc), lambda i, j: (jnp.maximum(i * sub - 1, 0), j)),
            pl.BlockSpec((SUBLANES, bc), lambda i, j: (jnp.minimum((i + 1) * sub, nsub - 1), j)),
            pl.BlockSpec((SUBLANES, bc), lambda i, j: (0, j)),
        ],
        out_specs=pl.BlockSpec((rb, bc), lambda i, j: (i, j)),
        out_shape=jax.ShapeDtypeStruct((t, d), jnp.bfloat16),
        compiler_params=_cparams(("parallel", "parallel"),
                                 6 * _nbytes((rb, bc), jnp.bfloat16) + 8 * _nbytes((rb, bc), jnp.float32)),
        name="short_conv",
    )(b_gate, u, u, u, w8)


def _conv_mixer(lay, h, w_in, conv_w):
    t, d = h.shape
    bm = lay.bm
    bn = _pick(d, 512)
    nn = d // bn
    o_sds = jax.ShapeDtypeStruct((t, d), jnp.bfloat16)
    b_gate, u = _matmul("conv_in", h, [(w_in, 0), (w_in, nn), (w_in, 2 * nn)], [],
                        [(o_sds, _tile_spec(bm, bn)), (o_sds, _tile_spec(bm, bn))],
                        _conv_in_epilogue, bm=bm, bn=bn, nn=nn,
                        epilogue_bytes=_nbytes((bm, bn), jnp.float32))
    return _short_conv(lay, b_gate, u, conv_w)


def kernel(x, c, ctx, c_ctx, ada_down, ada_up, ada_bias, norm1_g, norm2_g, s5_a_re, s5_a_im, s5_log_dt, s5_b_re, s5_b_im, s5_c_re, s5_c_im, s5_d, s5_w_glu, attn_w_qkv, attn_q_gain, attn_k_gain, attn_w_o, conv_w_in, conv_w, conv_w_out, mlp_w1, mlp_w2):
    batch, seq, d = x.shape
    n_ctx = ctx.shape[1]
    depth = ada_down.shape[0]
    lay = _Layout(batch, seq, n_ctx, d)
    bf16 = jnp.bfloat16

    xs = jnp.concatenate([x.reshape(lay.tl, d), ctx.reshape(lay.tc, d)], axis=0)

    cond = jnp.concatenate([c, c_ctx[None, :],
                            jnp.zeros((MOD_ROWS - batch - 1, d), c.dtype)], axis=0)
    mods = _ada_all(cond, ada_down, ada_up, ada_bias)
    mods = mods.reshape(depth, MOD_ROWS, N_MOD, 1, d).transpose(0, 2, 1, 3, 4)

    n_mixers = 3
    for i in range(depth):
        kind, j = i % n_mixers, i // n_mixers
        mod = mods[i]
        h = _norm_mod(lay, xs, norm1_g[i], mod, 0, 1)
        if kind == 0:
            params = _s5_prepare(s5_a_re[j], s5_a_im[j], s5_log_dt[j], s5_b_re[j], s5_b_im[j],
                                 s5_c_re[j], s5_c_im[j], s5_d[j])
            y = _s5_mixer(lay, h, params)
            xs = _gated_residual_matmul("s5_glu", lay, y, s5_w_glu[j].astype(bf16), xs, mod, 2,
                                        glu=True)
        elif kind == 1:
            dq = attn_w_o.shape[1]
            dkv = (attn_w_qkv.shape[2] - dq) // 2
            wqkv = attn_w_qkv[j]
            o = _attention_mixer(lay, h, wqkv[:, :dq].astype(bf16),
                                 wqkv[:, dq:dq + dkv].astype(bf16), wqkv[:, dq + dkv:].astype(bf16),
                                 attn_q_gain[j], attn_k_gain[j])
            xs = _gated_residual_matmul("attn_out", lay, o, attn_w_o[j].astype(bf16), xs, mod, 2)
        else:
            bu = _conv_mixer(lay, h, conv_w_in[j].astype(bf16), conv_w[j])
            xs = _gated_residual_matmul("conv_out", lay, bu, conv_w_out[j].astype(bf16), xs, mod, 2)
        h = _norm_mod(lay, xs, norm2_g[i], mod, 3, 4)
        xs = _mlp(lay, h, mlp_w1[i].astype(bf16), mlp_w2[i].astype(bf16), xs, mod)
    return xs[:lay.tl].reshape(batch, seq, d)
```

```python
import functools

import jax
import jax.numpy as jnp
from jax import lax
from jax.experimental import pallas as pl
from jax.experimental.pallas import tpu as pltpu

EPS = 1e-6
N_MOD = 6
S5_GROUP = 16
S5_CHUNK = 16
HEAD_DIM = 128
KV_REP = 4
GRID_W = 64
ROPE_THETA = 10000.0
LANES = 128
SUBLANES = 8
V7X_VMEM_BYTES = 64 * 1024 * 1024
VMEM_CEILING = V7X_VMEM_BYTES - 4 * 1024 * 1024
MOD_ROWS = 8

_HP = lax.Precision.HIGHEST


INTERNAL_SCRATCH_BYTES = 2 * 1024 * 1024


def _cparams(sem, vmem_bytes):
    vmem_bytes += INTERNAL_SCRATCH_BYTES
    return pltpu.CompilerParams(
        dimension_semantics=sem,
        vmem_limit_bytes=int(min(max(vmem_bytes, 16 * 1024 * 1024), VMEM_CEILING)))


def _nbytes(shape, dtype):
    n = 1
    for s in shape:
        if s is not None:
            n *= s
    return n * jnp.dtype(dtype).itemsize


def _pick(n, target, mult=LANES):
    best = None
    for d in range(mult, min(n, target) + 1, mult):
        if n % d == 0:
            best = d
    return best if best is not None else n


def _ada_kernel(cond_ref, down_ref, up_ref, b_ref, o_ref, t_ref):
    @pl.when(pl.program_id(1) == 0)
    def _():
        cnd = cond_ref[...]
        s = cnd * jax.nn.sigmoid(cnd)
        t_ref[...] = jnp.dot(s.astype(jnp.bfloat16), down_ref[...].astype(jnp.bfloat16),
                             preferred_element_type=jnp.float32)

    o_ref[...] = jnp.dot(t_ref[...].astype(jnp.bfloat16), up_ref[...].astype(jnp.bfloat16),
                         preferred_element_type=jnp.float32) + b_ref[...]


def _ada_all(cond, ada_down, ada_up, ada_bias):
    depth, d, r = ada_down.shape
    n = ada_up.shape[2]
    bn = _pick(n, 2048)
    vmem = 2 * (_nbytes((MOD_ROWS, d), jnp.float32) + _nbytes((d, r), jnp.float32)
                + _nbytes((r, bn), jnp.float32) + 2 * _nbytes((MOD_ROWS, bn), jnp.float32))
    return pl.pallas_call(
        _ada_kernel,
        grid=(depth, n // bn),
        in_specs=[
            pl.BlockSpec((MOD_ROWS, d), lambda l, j: (0, 0)),
            pl.BlockSpec((None, d, r), lambda l, j: (l, 0, 0)),
            pl.BlockSpec((None, r, bn), lambda l, j: (l, 0, j)),
            pl.BlockSpec((None, 1, bn), lambda l, j: (l, 0, j)),
        ],
        out_specs=pl.BlockSpec((None, MOD_ROWS, bn), lambda l, j: (l, 0, j)),
        out_shape=jax.ShapeDtypeStruct((depth, MOD_ROWS, n), jnp.float32),
        scratch_shapes=[pltpu.VMEM((MOD_ROWS, r), jnp.float32)],
        compiler_params=_cparams(("arbitrary", "arbitrary"), 2 * vmem),
        name="ada_mod",
    )(cond, ada_down, ada_up, ada_bias.reshape(depth, 1, n))


def _norm_kernel(x_ref, g_ref, sh_ref, sc_ref, o_ref):
    x = x_ref[...]
    ms = jnp.mean(x * x, axis=-1, keepdims=True)
    y = x * lax.rsqrt(ms + EPS) * g_ref[...]
    o_ref[...] = (y * (1.0 + sc_ref[...]) + sh_ref[...]).astype(o_ref.dtype)


class _Layout:
    def __init__(self, batch, seq, ctx, d):
        self.batch, self.seq, self.ctx, self.d = batch, seq, ctx, d
        self.tl, self.tc = batch * seq, batch * ctx
        self.t = self.tl + self.tc
        self.bm = _pick(self.tc, 1024, SUBLANES)
        assert seq % self.bm == 0 and self.tc % self.bm == 0
        assert batch + 1 <= MOD_ROWS

    def mod_row(self, i, bm):
        return jnp.where(i * bm < self.tl, (i * bm) // self.seq, self.batch)


def _norm_mod(lay, x, g, mod, shift_idx, scale_idx, out_dtype=jnp.bfloat16):
    t, d = x.shape
    bm = _pick(lay.bm, 256, SUBLANES)
    vmem = 2 * (_nbytes((bm, d), jnp.float32) + _nbytes((bm, d), out_dtype)) \
        + 4 * _nbytes((bm, d), jnp.float32)

    def mod_spec(which):
        return pl.BlockSpec((None, None, 1, d), lambda i: (which, lay.mod_row(i, bm), 0, 0))

    return pl.pallas_call(
        _norm_kernel,
        grid=(t // bm,),
        in_specs=[
            pl.BlockSpec((bm, d), lambda i: (i, 0)),
            pl.BlockSpec((1, d), lambda i: (0, 0)),
            mod_spec(shift_idx),
            mod_spec(scale_idx),
        ],
        out_specs=pl.BlockSpec((bm, d), lambda i: (i, 0)),
        out_shape=jax.ShapeDtypeStruct((t, d), out_dtype),
        compiler_params=_cparams(("parallel",), vmem),
        name="norm_mod",
    )(x, g.reshape(1, d), mod, mod)


def _tile(arr, bm, bn):
    return (arr, (bm, bn), lambda i, j: (i, j))


def _mod_op(lay, mod, which, bm, bn):
    return (mod, (None, None, 1, bn), lambda i, j: (which, lay.mod_row(i, bm), 0, j))


def _mm_kernel(*refs, n_extra, n_out, nk, epilogue):
    lhs_ref, rhs_ref = refs[0], refs[1]
    extra_refs = refs[2:2 + n_extra]
    out_refs = refs[2 + n_extra:2 + n_extra + n_out]
    acc_ref = refs[2 + n_extra + n_out]
    i, j, k = pl.program_id(0), pl.program_id(1), pl.program_id(2)

    @pl.when(k == 0)
    def _():
        acc_ref[...] = jnp.zeros_like(acc_ref)

    acc_ref[...] += jnp.dot(lhs_ref[...], rhs_ref[...], preferred_element_type=jnp.float32)

    @pl.when(k == nk - 1)
    def _():
        epilogue([acc_ref[...]], extra_refs, out_refs, i, j)


def _matmul_ktiled(name, lhs, w, extras, outs, epilogue, *, bm, bn, bk, epilogue_bytes=0):
    m, kdim = lhs.shape
    n = w.shape[1]
    nk = kdim // bk
    assert m % bm == 0 and kdim % bk == 0 and n % bn == 0

    def bind(shape, fn):
        return pl.BlockSpec(shape, lambda i, j, k: fn(i, j))

    vmem = 2 * _nbytes((bm, bk), lhs.dtype) + 2 * _nbytes((bk, bn), w.dtype)
    vmem += sum(2 * _nbytes(s, a.dtype) for a, s, _ in extras + outs)
    vmem += 3 * _nbytes((bm, bn), jnp.float32) + epilogue_bytes
    kern = functools.partial(_mm_kernel, n_extra=len(extras), n_out=len(outs), nk=nk,
                             epilogue=epilogue)
    return pl.pallas_call(
        kern,
        grid=(m // bm, n // bn, nk),
        in_specs=[pl.BlockSpec((bm, bk), lambda i, j, k: (i, k)),
                  pl.BlockSpec((bk, bn), lambda i, j, k: (k, j))]
        + [bind(s, f) for _, s, f in extras],
        out_specs=[bind(s, f) for _, s, f in outs],
        out_shape=[a for a, _, _ in outs],
        scratch_shapes=[pltpu.VMEM((bm, bn), jnp.float32)],
        compiler_params=_cparams(("parallel", "parallel", "arbitrary"), vmem),
        name=name,
    )(lhs, w, *[a for a, _, _ in extras])


def _mm_wres_kernel(*refs, n_rhs, n_extra, n_out, epilogue):
    lhs_ref = refs[0]
    w_refs = refs[1:1 + n_rhs]
    extra_refs = refs[1 + n_rhs:1 + n_rhs + n_extra]
    out_refs = refs[1 + n_rhs + n_extra:1 + n_rhs + n_extra + n_out]
    wbf_refs = refs[1 + n_rhs + n_extra + n_out:]
    j, i = pl.program_id(0), pl.program_id(1)

    @pl.when(i == 0)
    def _():
        for wb, w in zip(wbf_refs, w_refs):
            wb[...] = w[...].astype(wb.dtype)

    accs = [jnp.dot(lhs_ref[...], wb[...], preferred_element_type=jnp.float32) for wb in wbf_refs]
    epilogue(accs, extra_refs, out_refs, i, j)


def _matmul_wres(name, lhs, w, layer, offs, extras, outs, epilogue, *, bm, bn, nn,
                 epilogue_bytes=0):
    m, kdim = lhs.shape
    assert m % bm == 0 and w.shape[1] == kdim

    def bind(shape, fn):
        return pl.BlockSpec(shape, lambda j, i: fn(i, j))

    in_specs = [pl.BlockSpec((bm, kdim), lambda j, i: (i, 0))]
    for off in offs:
        in_specs.append(pl.BlockSpec((None, kdim, bn), lambda j, i, off=off: (layer, 0, off + j)))
    in_specs += [bind(s, f) for _, s, f in extras]
    vmem = 2 * _nbytes((bm, kdim), lhs.dtype)
    vmem += len(offs) * (2 * _nbytes((kdim, bn), w.dtype) + _nbytes((kdim, bn), jnp.bfloat16))
    vmem += sum(2 * _nbytes(s, a.dtype) for a, s, _ in extras + outs)
    vmem += 2 * len(offs) * _nbytes((bm, bn), jnp.float32) + epilogue_bytes
    kern = functools.partial(_mm_wres_kernel, n_rhs=len(offs), n_extra=len(extras),
                             n_out=len(outs), epilogue=epilogue)
    return pl.pallas_call(
        kern,
        grid=(nn, m // bm),
        in_specs=in_specs,
        out_specs=[bind(s, f) for _, s, f in outs],
        out_shape=[a for a, _, _ in outs],
        scratch_shapes=[pltpu.VMEM((kdim, bn), jnp.bfloat16) for _ in offs],
        compiler_params=_cparams(("arbitrary", "arbitrary"), vmem),
        name=name,
    )(lhs, *[w for _ in offs], *[a for a, _, _ in extras])


def _residual_epilogue(accs, extra_refs, out_refs, i, j):
    res_ref, gate_ref = extra_refs
    out_refs[0][...] = res_ref[...] + gate_ref[...] * accs[0]


def _glu_residual_epilogue(accs, extra_refs, out_refs, i, j):
    res_ref, gate_ref = extra_refs
    a, g = accs
    out_refs[0][...] = res_ref[...] + gate_ref[...] * (a * jax.nn.sigmoid(g))


def _relu2_epilogue(accs, extra_refs, out_refs, i, j):
    r = jnp.maximum(accs[0], 0.0)
    out_refs[0][...] = (r * r).astype(out_refs[0].dtype)


def _gated_residual_wres(name, lay, lhs, w, layer, x_res, mod, gate_idx, *, glu=False):
    t, d = x_res.shape
    bm = lay.bm
    bn = _pick(d, 256 if glu else 512)
    nn = d // bn
    extras = [_tile(x_res, bm, bn), _mod_op(lay, mod, gate_idx, bm, bn)]
    outs = [_tile(jax.ShapeDtypeStruct((t, d), jnp.float32), bm, bn)]
    epi = _glu_residual_epilogue if glu else _residual_epilogue
    return _matmul_wres(name, lhs, w, layer, [0, nn] if glu else [0], extras, outs, epi,
                        bm=bm, bn=bn, nn=nn, epilogue_bytes=2 * _nbytes((bm, bn), jnp.float32))[0]


def _mlp(lay, h, w1, w2_bf16, layer, x_res, mod):
    t, d = x_res.shape
    dff = w1.shape[2]
    bm = lay.bm
    bn = _pick(dff, 512)
    a = _matmul_wres("mlp_up", h, w1, layer, [0], [],
                     [_tile(jax.ShapeDtypeStruct((t, dff), jnp.bfloat16), bm, bn)],
                     _relu2_epilogue, bm=bm, bn=bn, nn=dff // bn,
                     epilogue_bytes=_nbytes((bm, bn), jnp.float32))[0]
    bn = _pick(d, 1024)
    return _matmul_ktiled("mlp_down", a, w2_bf16,
                          [_tile(x_res, bm, bn), _mod_op(lay, mod, 5, bm, bn)],
                          [_tile(jax.ShapeDtypeStruct((t, d), jnp.float32), bm, bn)],
                          _residual_epilogue, bm=bm, bn=bn, bk=_pick(dff, 2048),
                          epilogue_bytes=2 * _nbytes((bm, bn), jnp.float32))[0]


def _cmul(ar, ai, br, bi):
    return ar * br - ai * bi, ar * bi + ai * br


def _s5_prepare(a_re, a_im, log_dt, b_re, b_im, c_re, c_im, d_skip):
    f32 = jnp.float32
    L = S5_CHUNK
    a_re = jnp.minimum(a_re.astype(f32), -1e-4)
    a_im = a_im.astype(f32)
    dt = jnp.exp(log_dt.astype(f32))[..., None]
    mag = jnp.exp(a_re * dt)
    ab_re = mag * jnp.cos(a_im * dt)
    ab_im = mag * jnp.sin(a_im * dt)
    den = a_re * a_re + a_im * a_im
    n_re = ab_re - 1.0
    q_re = (n_re * a_re + ab_im * a_im) / den
    q_im = (ab_im * a_re - n_re * a_im) / den
    b_re, b_im = b_re.astype(f32), b_im.astype(f32)
    bb_re = q_re[..., None] * b_re - q_im[..., None] * b_im
    bb_im = q_re[..., None] * b_im + q_im[..., None] * b_re
    c_re, c_im = c_re.astype(f32), c_im.astype(f32)
    bb_re, bb_im, c_re, c_im = [v.astype(jnp.bfloat16).astype(f32) for v in (bb_re, bb_im, c_re, c_im)]
    ngroups, nstate = ab_re.shape[1], ab_re.shape[2]
    cg = b_re.shape[-1]
    gpb = LANES // cg
    ncb = ngroups // gpb
    bf16 = jnp.bfloat16

    pr, pi = [jnp.ones_like(ab_re)], [jnp.zeros_like(ab_im)]
    for _ in range(L):
        nr, ni = _cmul(pr[-1], pi[-1], ab_re, ab_im)
        pr.append(nr)
        pi.append(ni)
    pw_re, pw_im = jnp.stack(pr), jnp.stack(pi)

    cp_re, cp_im = _cmul(c_re[None], c_im[None], pw_re[:L, :, :, None, :], pw_im[:L, :, :, None, :])
    kern = (jnp.einsum('kdgcp,dgpe->kdgce', cp_re, bb_re, precision=_HP)
            - jnp.einsum('kdgcp,dgpe->kdgce', cp_im, bb_im, precision=_HP))
    kf, kb = kern[:, 0], kern[:, 1]
    diag = jnp.eye(cg, dtype=f32)[None] * d_skip.astype(f32).reshape(ngroups, cg)[:, :, None]
    k_all = jnp.concatenate([kb[:0:-1], (kf[0] + kb[0] + diag)[None], kf[1:]], axis=0)
    s_idx = jnp.arange(L)[:, None]
    t_idx = jnp.arange(L)[None, :]
    tm = k_all[t_idx - s_idx + (L - 1)]
    tm = tm.transpose(2, 0, 4, 1, 3).astype(bf16)
    eye = jnp.eye(gpb, dtype=bf16)
    tm = tm.reshape(ncb, gpb, L, cg, L, cg)
    t_cat = (tm.transpose(0, 2, 1, 3, 4, 5)[:, :, :, :, :, None, :]
             * eye[None, None, :, None, None, :, None]).reshape(ncb, L * LANES, L * LANES)

    def win_dir(d, pows_re, pows_im):
        wr, wi = _cmul(pows_re[:, :, None, :], pows_im[:, :, None, :],
                       jnp.swapaxes(bb_re[d], 1, 2)[None], jnp.swapaxes(bb_im[d], 1, 2)[None])
        return wr, wi

    def pows(ks, d):
        return jnp.stack([pr[k][d] for k in ks]), jnp.stack([pi[k][d] for k in ks])

    wf_re, wf_im = win_dir(0, *pows(range(L - 1, -1, -1), 0))
    wb_re, wb_im = win_dir(1, *pows(range(L), 1))
    w4 = jnp.stack([wf_re, wf_im, wb_re, wb_im], axis=3).astype(bf16)
    w4 = w4.reshape(L, ncb, gpb, cg, 4, nstate)
    win_cat = (w4.transpose(1, 0, 2, 3, 4, 5)[:, :, :, :, :, None, :]
               * eye[None, None, :, None, None, :, None]).reshape(ncb, L * LANES, 4 * gpb * nstate)

    def wout_dir(d, pows_re, pows_im):
        vr, vi = _cmul(c_re[d][None], c_im[d][None], pows_re[:, :, None, :], pows_im[:, :, None, :])
        return vr, vi

    vf_re, vf_im = wout_dir(0, *pows(range(1, L + 1), 0))
    vb_re, vb_im = wout_dir(1, *pows(range(L, 0, -1), 1))
    v4 = jnp.stack([vf_re, -vf_im, vb_re, -vb_im], axis=0).astype(bf16)
    v4 = v4.reshape(4, L, ncb, gpb, cg, nstate)
    wout_cat = (v4.transpose(2, 0, 5, 1, 3, 4)[:, :, None, :, :, :, :]
                * eye[None, None, :, None, None, :, None]).reshape(ncb, 4 * gpb * nstate, L * LANES)

    a4 = jnp.stack([pw_re[L, 0], pw_im[L, 0], pw_re[L, 1], pw_im[L, 1]], axis=0)
    a4 = a4.reshape(4, ncb, gpb * nstate).transpose(1, 0, 2)
    a8 = jnp.concatenate([a4, jnp.zeros_like(a4)], axis=1)
    return t_cat, win_cat, wout_cat, a8


def _s5_kernel(hl_ref, hc_ref, wz_ref, wt_ref, wo_ref, a_ref, yl_ref, yc_ref, u_ref, z_ref,
               *, jl, jc):
    L, lanes = hl_ref.shape[1], hl_ref.shape[2]
    for t in range(L):
        u_ref[0:jl, t * lanes:(t + 1) * lanes] = hl_ref[:, t, :].astype(u_ref.dtype)
        u_ref[jl:jl + jc, t * lanes:(t + 1) * lanes] = hc_ref[:, t, :].astype(u_ref.dtype)
    z_ref[...] = jnp.dot(u_ref[...], wz_ref[...], preferred_element_type=jnp.float32)

    sw = z_ref.shape[1] // 4
    a = a_ref[...]
    afr, afi, abr, abi = a[0:1, :], a[1:2, :], a[2:3, :], a[3:4, :]
    rpb = SUBLANES

    def sweep(n, base, carry):
        nblk = n // rpb

        def body(i, carry):
            sfr, sfi, sbr, sbi = carry
            rf = pl.multiple_of(base + i * rpb, rpb)
            rb = pl.multiple_of(base + (nblk - 1 - i) * rpb, rpb)
            zfr = z_ref[pl.ds(rf, rpb), 0 * sw:1 * sw]
            zfi = z_ref[pl.ds(rf, rpb), 1 * sw:2 * sw]
            zbr = z_ref[pl.ds(rb, rpb), 2 * sw:3 * sw]
            zbi = z_ref[pl.ds(rb, rpb), 3 * sw:4 * sw]
            pfr, pfi, pbr, pbi = [], [], [], []
            for c in range(rpb):
                d = rpb - 1 - c
                pfr.append(sfr)
                pfi.append(sfi)
                pbr.insert(0, sbr)
                pbi.insert(0, sbi)
                sfr, sfi = (afr * sfr - afi * sfi + zfr[c:c + 1], afr * sfi + afi * sfr + zfi[c:c + 1])
                sbr, sbi = (abr * sbr - abi * sbi + zbr[d:d + 1], abr * sbi + abi * sbr + zbi[d:d + 1])
            z_ref[pl.ds(rf, rpb), 0 * sw:1 * sw] = jnp.concatenate(pfr, axis=0)
            z_ref[pl.ds(rf, rpb), 1 * sw:2 * sw] = jnp.concatenate(pfi, axis=0)
            z_ref[pl.ds(rb, rpb), 2 * sw:3 * sw] = jnp.concatenate(pbr, axis=0)
            z_ref[pl.ds(rb, rpb), 3 * sw:4 * sw] = jnp.concatenate(pbi, axis=0)
            return sfr, sfi, sbr, sbi

        return lax.fori_loop(0, nblk, body, carry)

    zero = jnp.zeros((1, sw), jnp.float32)
    carry = sweep(jc, jl, (zero, zero, zero, zero))
    sweep(jl, 0, carry)

    y = jnp.dot(u_ref[...], wt_ref[...], preferred_element_type=jnp.float32)
    y = y + jnp.dot(z_ref[...].astype(jnp.bfloat16), wo_ref[...], preferred_element_type=jnp.float32)
    y = jax.nn.gelu(y)
    for t in range(L):
        yl_ref[:, t, :] = y[0:jl, t * lanes:(t + 1) * lanes].astype(yl_ref.dtype)
        yc_ref[:, t, :] = y[jl:jl + jc, t * lanes:(t + 1) * lanes].astype(yc_ref.dtype)


def _s5_mixer(lay, h, params, out_dtype):
    t_cat, win_cat, wout_cat, a8 = params
    b, d = lay.batch, lay.d
    L = S5_CHUNK
    jl, jc = lay.seq // L, lay.ctx // L
    assert jl % SUBLANES == 0 and jc % SUBLANES == 0 and (b * jl) % jc == 0
    cw = L * LANES
    sw4 = win_cat.shape[2]
    rows = jl + jc
    h3 = h.reshape(lay.t // L, L, d)
    ctx0 = (b * jl) // jc
    once = pl.Buffered(1)
    vmem = 2 * _nbytes((rows, L, LANES), h.dtype) + 2 * _nbytes((rows, L, LANES), out_dtype) \
        + 2 * _nbytes((cw, cw), jnp.bfloat16) + _nbytes((cw, sw4), jnp.bfloat16) \
        + _nbytes((rows, cw), jnp.bfloat16) + 4 * _nbytes((rows, max(cw, sw4)), jnp.float32)
    yl, yc = pl.pallas_call(
        functools.partial(_s5_kernel, jl=jl, jc=jc),
        grid=(d // LANES, b),
        in_specs=[
            pl.BlockSpec((jl, L, LANES), lambda cb, bi: (bi, 0, cb)),
            pl.BlockSpec((jc, L, LANES), lambda cb, bi: (ctx0 + bi, 0, cb)),
            pl.BlockSpec((None, cw, sw4), lambda cb, bi: (cb, 0, 0), pipeline_mode=once),
            pl.BlockSpec((None, cw, cw), lambda cb, bi: (cb, 0, 0), pipeline_mode=once),
            pl.BlockSpec((None, sw4, cw), lambda cb, bi: (cb, 0, 0), pipeline_mode=once),
            pl.BlockSpec((None, SUBLANES, a8.shape[2]), lambda cb, bi: (cb, 0, 0)),
        ],
        out_specs=[pl.BlockSpec((jl, L, LANES), lambda cb, bi: (bi, 0, cb)),
                   pl.BlockSpec((jc, L, LANES), lambda cb, bi: (bi, 0, cb))],
        out_shape=[jax.ShapeDtypeStruct((b * jl, L, d), out_dtype),
                   jax.ShapeDtypeStruct((b * jc, L, d), out_dtype)],
        scratch_shapes=[pltpu.VMEM((rows, cw), jnp.bfloat16), pltpu.VMEM((rows, sw4), jnp.float32)],
        compiler_params=_cparams(("arbitrary", "arbitrary"), vmem),
        name="s5_chunked",
    )(h3, h3, win_cat, t_cat, wout_cat, a8)
    y = jnp.concatenate([yl.reshape(lay.tl, d), yc.reshape(lay.tc, d)], axis=0)
    return y.astype(jnp.bfloat16)


def _rope_tables(lay, bm):
    n = lay.seq
    rows = n // GRID_W
    row = jnp.broadcast_to(jnp.arange(rows)[:, None], (rows, GRID_W)).reshape(-1)
    col = jnp.broadcast_to(jnp.arange(GRID_W)[None, :], (rows, GRID_W)).reshape(-1)
    n_freq = HEAD_DIM // 4
    inv_freq = ROPE_THETA ** (-jnp.arange(n_freq, dtype=jnp.float32) / n_freq)
    ang_r = row.astype(jnp.float32)[:, None] * inv_freq
    ang_c = col.astype(jnp.float32)[:, None] * inv_freq
    cr, sr, cc, sc = jnp.cos(ang_r), jnp.sin(ang_r), jnp.cos(ang_c), jnp.sin(ang_c)
    cos = jnp.concatenate([cr, cr, cc, cc], axis=-1)
    sin = jnp.concatenate([-sr, sr, -sc, sc], axis=-1)
    cos = jnp.concatenate([cos, jnp.ones((bm, HEAD_DIM), jnp.float32)], axis=0)
    sin = jnp.concatenate([sin, jnp.zeros((bm, HEAD_DIM), jnp.float32)], axis=0)
    return cos, sin


def _norm_rope(x, gain, cos, sin, scale):
    ms = jnp.mean(x * x, axis=-1, keepdims=True)
    xn = x * lax.rsqrt(ms + EPS) * gain
    q4 = HEAD_DIM // 4
    lane = lax.broadcasted_iota(jnp.int32, xn.shape, 1)
    partner = jnp.where(lane % (2 * q4) < q4,
                        pltpu.roll(xn, HEAD_DIM - q4, 1),
                        pltpu.roll(xn, q4, 1))
    out = xn * cos + partner * sin
    return out * scale if scale != 1.0 else out


def _q_epilogue(accs, extra_refs, out_refs, i, j, *, scale):
    gain_ref, cos_ref, sin_ref = extra_refs
    acc = accs[0]
    cos, sin, gain = cos_ref[...], sin_ref[...], gain_ref[...]
    for hh in range(acc.shape[1] // HEAD_DIM):
        sl = slice(hh * HEAD_DIM, (hh + 1) * HEAD_DIM)
        out_refs[0][:, sl] = _norm_rope(acc[:, sl], gain, cos, sin, scale).astype(out_refs[0].dtype)


def _kv_epilogue(accs, extra_refs, out_refs, i, j):
    gain_ref, cos_ref, sin_ref = extra_refs
    k_acc, v_acc = accs
    cos, sin, gain = cos_ref[...], sin_ref[...], gain_ref[...]
    for hh in range(k_acc.shape[1] // HEAD_DIM):
        sl = slice(hh * HEAD_DIM, (hh + 1) * HEAD_DIM)
        out_refs[0][:, sl] = _norm_rope(k_acc[:, sl], gain, cos, sin, 1.0).astype(out_refs[0].dtype)
    out_refs[1][...] = v_acc.astype(out_refs[1].dtype)


def _attn_kernel(q_ref, kx_ref, kc_ref, vx_ref, vc_ref, o_ref, *, with_latent):
    dn = (((1,), (1,)), ((), ()))
    for r in range(KV_REP):
        sl = slice(r * HEAD_DIM, (r + 1) * HEAD_DIM)
        q = q_ref[:, sl]
        sc = lax.dot_general(q, kc_ref[...], dn, preferred_element_type=jnp.float32)
        m = jnp.max(sc, axis=-1, keepdims=True)
        if with_latent:
            sx = lax.dot_general(q, kx_ref[...], dn, preferred_element_type=jnp.float32)
            m = jnp.maximum(m, jnp.max(sx, axis=-1, keepdims=True))
            px = jnp.exp(sx - m)
        pc = jnp.exp(sc - m)
        l = jnp.sum(pc, axis=-1, keepdims=True)
        o = jnp.dot(pc.astype(vc_ref.dtype), vc_ref[...], preferred_element_type=jnp.float32)
        if with_latent:
            l = l + jnp.sum(px, axis=-1, keepdims=True)
            o = o + jnp.dot(px.astype(vx_ref.dtype), vx_ref[...], preferred_element_type=jnp.float32)
        o_ref[:, sl] = (o / l).astype(o_ref.dtype)


def _attention(lay, q, k, v):
    t, dq = q.shape
    nkv = k.shape[1] // HEAD_DIM
    b, seq, ctx = lay.batch, lay.seq, lay.ctx
    qw = KV_REP * HEAD_DIM
    tq = _pick(seq, 256, SUBLANES)
    nq = seq // tq
    cblk0 = lay.tl // ctx
    kx_spec = pl.BlockSpec((seq, HEAD_DIM), lambda bi, g, qi: (bi, g))
    kc_spec = pl.BlockSpec((ctx, HEAD_DIM), lambda bi, g, qi: (cblk0 + bi, g))
    vmem = 4 * _nbytes((tq, qw), jnp.bfloat16) + 8 * _nbytes((seq + ctx, HEAD_DIM), jnp.bfloat16) \
        + 4 * _nbytes((tq, seq + ctx), jnp.float32)
    o_lat = pl.pallas_call(
        functools.partial(_attn_kernel, with_latent=True),
        grid=(b, nkv, nq),
        in_specs=[pl.BlockSpec((tq, qw), lambda bi, g, qi: (bi * nq + qi, g)),
                  kx_spec, kc_spec, kx_spec, kc_spec],
        out_specs=pl.BlockSpec((tq, qw), lambda bi, g, qi: (bi * nq + qi, g)),
        out_shape=jax.ShapeDtypeStruct((t, dq), jnp.bfloat16),
        compiler_params=_cparams(("parallel", "parallel", "arbitrary"), vmem),
        name="attn_latent",
    )(q, k, k, v, v)
    kc2 = pl.BlockSpec((ctx, HEAD_DIM), lambda bi, g: (cblk0 + bi, g))

    def ctx_kernel(q_ref, kc_ref, vc_ref, prev_ref, o_ref):
        del prev_ref
        _attn_kernel(q_ref, None, kc_ref, None, vc_ref, o_ref, with_latent=False)

    return pl.pallas_call(
        ctx_kernel,
        grid=(b, nkv),
        in_specs=[pl.BlockSpec((ctx, qw), lambda bi, g: (cblk0 + bi, g)), kc2, kc2,
                  pl.BlockSpec(memory_space=pl.ANY)],
        out_specs=pl.BlockSpec((ctx, qw), lambda bi, g: (cblk0 + bi, g)),
        out_shape=jax.ShapeDtypeStruct((t, dq), jnp.bfloat16),
        input_output_aliases={3: 0},
        compiler_params=_cparams(("parallel", "parallel"), 16 * 1024 * 1024),
        name="attn_context",
    )(q, k, v, o_lat)


def _attention_mixer(lay, h, w_qkv, layer, dq, dkv, q_gain, k_gain):
    t, d = h.shape
    bm = lay.bm
    cos, sin = _rope_tables(lay, bm)
    nseq = lay.seq // bm

    def pos_block(i):
        return jnp.where(i * bm < lay.tl, i % nseq, nseq)

    tab = lambda arr: (arr, (bm, HEAD_DIM), lambda i, j: (pos_block(i), 0))
    gain = lambda g: (g.reshape(1, HEAD_DIM), (1, HEAD_DIM), lambda i, j: (0, 0))
    bn = _pick(dq, 512)
    q = _matmul_wres("attn_q", h, w_qkv, layer, [0], [gain(q_gain), tab(cos), tab(sin)],
                     [_tile(jax.ShapeDtypeStruct((t, dq), jnp.bfloat16), bm, bn)],
                     functools.partial(_q_epilogue, scale=HEAD_DIM ** -0.5),
                     bm=bm, bn=bn, nn=dq // bn, epilogue_bytes=4 * _nbytes((bm, bn), jnp.float32))[0]
    bn = _pick(dkv, 256)
    o_sds = jax.ShapeDtypeStruct((t, dkv), jnp.bfloat16)
    k, v = _matmul_wres("attn_kv", h, w_qkv, layer, [dq // bn, (dq + dkv) // bn],
                        [gain(k_gain), tab(cos), tab(sin)],
                        [_tile(o_sds, bm, bn), _tile(o_sds, bm, bn)],
                        _kv_epilogue, bm=bm, bn=bn, nn=dkv // bn,
                        epilogue_bytes=4 * _nbytes((bm, bn), jnp.float32))
    return _attention(lay, q, k, v)


def _conv_in_epilogue(accs, extra_refs, out_refs, i, j):
    b_gate, c_gate, v = accs
    out_refs[0][...] = b_gate.astype(out_refs[0].dtype)
    out_refs[1][...] = (c_gate * v).astype(out_refs[1].dtype)


def _conv_kernel(b_ref, u_ref, up_ref, un_ref, w_ref, o_ref, *, rb, lay):
    i = pl.program_id(0)
    row0 = i * rb
    is_lat = row0 < lay.tl
    seq_len = jnp.where(is_lat, lay.seq, lay.ctx)
    local = jnp.where(is_lat, row0, row0 - lay.tl) % seq_len
    first = local == 0
    last = local + rb == seq_len
    u = u_ref[...].astype(jnp.float32)
    prev_row = jnp.where(first, 0.0, up_ref[SUBLANES - 1:SUBLANES, :].astype(jnp.float32))
    next_row = jnp.where(last, 0.0, un_ref[0:1, :].astype(jnp.float32))
    ridx = lax.broadcasted_iota(jnp.int32, u.shape, 0)
    u_prev = jnp.where(ridx == 0, prev_row, pltpu.roll(u, 1, 0))
    u_next = jnp.where(ridx == rb - 1, next_row, pltpu.roll(u, rb - 1, 0))
    w = w_ref[...]
    conv = w[0:1, :] * u_prev + w[1:2, :] * u + w[2:3, :] * u_next
    o_ref[...] = (b_ref[...].astype(jnp.float32) * conv).astype(o_ref.dtype)


def _short_conv(lay, b_gate, u, conv_w):
    t, d = u.shape
    rb = _pick(lay.ctx, 256, SUBLANES)
    assert lay.ctx % rb == 0 and lay.seq % rb == 0
    bc = _pick(d, 1024)
    sub = rb // SUBLANES
    nsub = t // SUBLANES
    w8 = jnp.concatenate([conv_w.astype(jnp.float32),
                          jnp.zeros((SUBLANES - conv_w.shape[0], d), jnp.float32)], axis=0)
    return pl.pallas_call(
        functools.partial(_conv_kernel, rb=rb, lay=lay),
        grid=(t // rb, d // bc),
        in_specs=[
            pl.BlockSpec((rb, bc), lambda i, j: (i, j)),
            pl.BlockSpec((rb, bc), lambda i, j: (i, j)),
            pl.BlockSpec((SUBLANES, bc), lambda i, j: (jnp.maximum(i * sub - 1, 0), j)),
            pl.BlockSpec((SUBLANES, bc), lambda i, j: (jnp.minimum((i + 1) * sub, nsub - 1), j)),
            pl.BlockSpec((SUBLANES, bc), lambda i, j: (0, j)),
        ],
        out_specs=pl.BlockSpec((rb, bc), lambda i, j: (i, j)),
        out_shape=jax.ShapeDtypeStruct((t, d), jnp.bfloat16),
        compiler_params=_cparams(("parallel", "parallel"),
                                 6 * _nbytes((rb, bc), jnp.bfloat16) + 8 * _nbytes((rb, bc), jnp.float32)),
        name="short_conv",
    )(b_gate, u, u, u, w8)


def _conv_mixer(lay, h, w_in, layer, conv_w):
    t, d = h.shape
    bm = _pick(lay.bm, 512, SUBLANES)
    bn = _pick(d, 256)
    nn = d // bn
    o_sds = jax.ShapeDtypeStruct((t, d), jnp.bfloat16)
    b_gate, u = _matmul_wres("conv_in", h, w_in, layer, [0, nn, 2 * nn], [],
                             [_tile(o_sds, bm, bn), _tile(o_sds, bm, bn)],
                             _conv_in_epilogue, bm=bm, bn=bn, nn=nn,
                             epilogue_bytes=_nbytes((bm, bn), jnp.float32))
    return _short_conv(lay, b_gate, u, conv_w)


def kernel(x, c, ctx, c_ctx, ada_down, ada_up, ada_bias, norm1_g, norm2_g, s5_a_re, s5_a_im, s5_log_dt, s5_b_re, s5_b_im, s5_c_re, s5_c_im, s5_d, s5_w_glu, attn_w_qkv, attn_q_gain, attn_k_gain, attn_w_o, conv_w_in, conv_w, conv_w_out, mlp_w1, mlp_w2):
    batch, seq, d = x.shape
    n_ctx = ctx.shape[1]
    depth = ada_down.shape[0]
    lay = _Layout(batch, seq, n_ctx, d)
    bf16 = jnp.bfloat16

    xs = jnp.concatenate([x.reshape(lay.tl, d), ctx.reshape(lay.tc, d)], axis=0)

    cond = jnp.concatenate([c, c_ctx[None, :],
                            jnp.zeros((MOD_ROWS - batch - 1, d), c.dtype)], axis=0)
    mods = _ada_all(cond, ada_down, ada_up, ada_bias)
    mods = mods.reshape(depth, MOD_ROWS, N_MOD, 1, d).transpose(0, 2, 1, 3, 4)

    n_mixers = 3
    for i in range(depth):
        kind, j = i % n_mixers, i // n_mixers
        mod = mods[i]
        h = _norm_mod(lay, xs, norm1_g[i], mod, 0, 1,
                      out_dtype=jnp.float32 if kind == 0 else bf16)
        if kind == 0:
            params = _s5_prepare(s5_a_re[j], s5_a_im[j], s5_log_dt[j], s5_b_re[j], s5_b_im[j],
                                 s5_c_re[j], s5_c_im[j], s5_d[j])
            y = _s5_mixer(lay, h, params, jnp.float32)
            xs = _gated_residual_wres("s5_glu", lay, y, s5_w_glu, j, xs, mod, 2, glu=True)
        elif kind == 1:
            dq = attn_w_o.shape[1]
            dkv = (attn_w_qkv.shape[2] - dq) // 2
            o = _attention_mixer(lay, h, attn_w_qkv, j, dq, dkv, attn_q_gain[j], attn_k_gain[j])
            xs = _gated_residual_wres("attn_out", lay, o, attn_w_o, j, xs, mod, 2)
        else:
            bu = _conv_mixer(lay, h, conv_w_in, j, conv_w[j])
            xs = _gated_residual_wres("conv_out", lay, bu, conv_w_out, j, xs, mod, 2)
        h = _norm_mod(lay, xs, norm2_g[i], mod, 3, 4)
        xs = _mlp(lay, h, mlp_w1, mlp_w2[i].astype(bf16), i, xs, mod)
    return xs[:lay.tl].reshape(batch, seq, d)
```

```python
import functools

import jax
import jax.numpy as jnp
from jax import lax
from jax.experimental import pallas as pl
from jax.experimental.pallas import tpu as pltpu

EPS = 1e-6
N_MOD = 6
S5_GROUP = 16
S5_CHUNK = 16
HEAD_DIM = 128
KV_REP = 4
GRID_W = 64
ROPE_THETA = 10000.0
LANES = 128
SUBLANES = 8
V7X_VMEM_BYTES = 64 * 1024 * 1024
VMEM_CEILING = V7X_VMEM_BYTES - 4 * 1024 * 1024
MOD_ROWS = 8

_HP = lax.Precision.HIGHEST


INTERNAL_SCRATCH_BYTES = 2 * 1024 * 1024


def _cparams(sem, vmem_bytes):
    vmem_bytes += INTERNAL_SCRATCH_BYTES
    return pltpu.CompilerParams(
        dimension_semantics=sem,
        vmem_limit_bytes=int(min(max(vmem_bytes, 16 * 1024 * 1024), VMEM_CEILING)))


def _nbytes(shape, dtype):
    n = 1
    for s in shape:
        if s is not None:
            n *= s
    return n * jnp.dtype(dtype).itemsize


def _pick(n, target, mult=LANES):
    best = None
    for d in range(mult, min(n, target) + 1, mult):
        if n % d == 0:
            best = d
    return best if best is not None else n


def _ada_kernel(cond_ref, down_ref, up_ref, b_ref, o_ref, t_ref):
    @pl.when(pl.program_id(1) == 0)
    def _():
        cnd = cond_ref[...]
        s = cnd * jax.nn.sigmoid(cnd)
        t_ref[...] = jnp.dot(s.astype(jnp.bfloat16), down_ref[...].astype(jnp.bfloat16),
                             preferred_element_type=jnp.float32)

    o_ref[...] = jnp.dot(t_ref[...].astype(jnp.bfloat16), up_ref[...].astype(jnp.bfloat16),
                         preferred_element_type=jnp.float32) + b_ref[...]


def _ada_all(cond, ada_down, ada_up, ada_bias):
    depth, d, r = ada_down.shape
    n = ada_up.shape[2]
    bn = _pick(n, 2048)
    vmem = 2 * (_nbytes((MOD_ROWS, d), jnp.float32) + _nbytes((d, r), jnp.float32)
                + _nbytes((r, bn), jnp.float32) + 2 * _nbytes((MOD_ROWS, bn), jnp.float32))
    return pl.pallas_call(
        _ada_kernel,
        grid=(depth, n // bn),
        in_specs=[
            pl.BlockSpec((MOD_ROWS, d), lambda l, j: (0, 0)),
            pl.BlockSpec((None, d, r), lambda l, j: (l, 0, 0)),
            pl.BlockSpec((None, r, bn), lambda l, j: (l, 0, j)),
            pl.BlockSpec((None, 1, bn), lambda l, j: (l, 0, j)),
        ],
        out_specs=pl.BlockSpec((None, MOD_ROWS, bn), lambda l, j: (l, 0, j)),
        out_shape=jax.ShapeDtypeStruct((depth, MOD_ROWS, n), jnp.float32),
        scratch_shapes=[pltpu.VMEM((MOD_ROWS, r), jnp.float32)],
        compiler_params=_cparams(("arbitrary", "arbitrary"), 2 * vmem),
        name="ada_mod",
    )(cond, ada_down, ada_up, ada_bias.reshape(depth, 1, n))


def _norm_kernel(x_ref, g_ref, sh_ref, sc_ref, o_ref):
    x = x_ref[...]
    ms = jnp.mean(x * x, axis=-1, keepdims=True)
    y = x * lax.rsqrt(ms + EPS) * g_ref[...]
    o_ref[...] = (y * (1.0 + sc_ref[...]) + sh_ref[...]).astype(o_ref.dtype)


class _Layout:
    def __init__(self, batch, seq, ctx, d):
        self.batch, self.seq, self.ctx, self.d = batch, seq, ctx, d
        self.tl, self.tc = batch * seq, batch * ctx
        self.t = self.tl + self.tc
        self.bm = _pick(self.tc, 1024, SUBLANES)
        assert seq % self.bm == 0 and self.tc % self.bm == 0
        assert batch + 1 <= MOD_ROWS

    def mod_row(self, i, bm):
        return jnp.where(i * bm < self.tl, (i * bm) // self.seq, self.batch)


def _norm_mod(lay, x, g, mod, shift_idx, scale_idx, out_dtype=jnp.bfloat16):
    t, d = x.shape
    bm = _pick(lay.bm, 256, SUBLANES)
    vmem = 2 * (_nbytes((bm, d), jnp.float32) + _nbytes((bm, d), out_dtype)) \
        + 4 * _nbytes((bm, d), jnp.float32)

    def mod_spec(which):
        return pl.BlockSpec((None, None, 1, d), lambda i: (which, lay.mod_row(i, bm), 0, 0))

    return pl.pallas_call(
        _norm_kernel,
        grid=(t // bm,),
        in_specs=[
            pl.BlockSpec((bm, d), lambda i: (i, 0)),
            pl.BlockSpec((1, d), lambda i: (0, 0)),
            mod_spec(shift_idx),
            mod_spec(scale_idx),
        ],
        out_specs=pl.BlockSpec((bm, d), lambda i: (i, 0)),
        out_shape=jax.ShapeDtypeStruct((t, d), out_dtype),
        compiler_params=_cparams(("parallel",), vmem),
        name="norm_mod",
    )(x, g.reshape(1, d), mod, mod)


def _tile(arr, bm, bn):
    return (arr, (bm, bn), lambda i, j: (i, j))


def _mod_op(lay, mod, which, bm, bn):
    return (mod, (None, None, 1, bn), lambda i, j: (which, lay.mod_row(i, bm), 0, j))


def _mm_wres_kernel(*refs, n_rhs, n_extra, n_out, epilogue):
    lhs_ref = refs[0]
    w_refs = refs[1:1 + n_rhs]
    extra_refs = refs[1 + n_rhs:1 + n_rhs + n_extra]
    out_refs = refs[1 + n_rhs + n_extra:1 + n_rhs + n_extra + n_out]
    wbf_refs = refs[1 + n_rhs + n_extra + n_out:]
    j, i = pl.program_id(0), pl.program_id(1)

    @pl.when(i == 0)
    def _():
        for wb, w in zip(wbf_refs, w_refs):
            wb[...] = w[...].astype(wb.dtype)

    accs = [jnp.dot(lhs_ref[...], wb[...], preferred_element_type=jnp.float32) for wb in wbf_refs]
    epilogue(accs, extra_refs, out_refs, i, j)


def _matmul_wres(name, lhs, w, layer, offs, extras, outs, epilogue, *, bm, bn, nn,
                 kblk=None, kidx=0, epilogue_bytes=0):
    m = lhs.shape[0]
    kdim = lhs.shape[1] if kblk is None else kblk
    assert m % bm == 0 and w.shape[1] == lhs.shape[1] and lhs.shape[1] % kdim == 0

    def bind(shape, fn):
        return pl.BlockSpec(shape, lambda j, i: fn(i, j))

    in_specs = [pl.BlockSpec((bm, kdim), lambda j, i: (i, kidx))]
    for off in offs:
        in_specs.append(pl.BlockSpec((None, kdim, bn), lambda j, i, off=off: (layer, kidx, off + j)))
    in_specs += [bind(s, f) for _, s, f in extras]
    vmem = 2 * _nbytes((bm, kdim), lhs.dtype)
    vmem += len(offs) * (2 * _nbytes((kdim, bn), w.dtype) + _nbytes((kdim, bn), jnp.bfloat16))
    vmem += sum(2 * _nbytes(s, a.dtype) for a, s, _ in extras + outs)
    vmem += 2 * len(offs) * _nbytes((bm, bn), jnp.float32) + epilogue_bytes
    kern = functools.partial(_mm_wres_kernel, n_rhs=len(offs), n_extra=len(extras),
                             n_out=len(outs), epilogue=epilogue)
    return pl.pallas_call(
        kern,
        grid=(nn, m // bm),
        in_specs=in_specs,
        out_specs=[bind(s, f) for _, s, f in outs],
        out_shape=[a for a, _, _ in outs],
        scratch_shapes=[pltpu.VMEM((kdim, bn), jnp.bfloat16) for _ in offs],
        compiler_params=_cparams(("arbitrary", "arbitrary"), vmem),
        name=name,
    )(lhs, *[w for _ in offs], *[a for a, _, _ in extras])


def _residual_epilogue(accs, extra_refs, out_refs, i, j):
    res_ref, gate_ref = extra_refs
    out_refs[0][...] = res_ref[...] + gate_ref[...] * accs[0]


def _glu_residual_epilogue(accs, extra_refs, out_refs, i, j):
    res_ref, gate_ref = extra_refs
    a, g = accs
    out_refs[0][...] = res_ref[...] + gate_ref[...] * (a * jax.nn.sigmoid(g))


def _relu2_epilogue(accs, extra_refs, out_refs, i, j):
    r = jnp.maximum(accs[0], 0.0)
    out_refs[0][...] = (r * r).astype(out_refs[0].dtype)


def _gated_residual_wres(name, lay, lhs, w, layer, x_res, mod, gate_idx, *, glu=False,
                         kblk=None, kidx=0):
    t, d = x_res.shape
    bm = lay.bm
    bn = _pick(d, 256 if glu else 512)
    nn = d // bn
    extras = [_tile(x_res, bm, bn), _mod_op(lay, mod, gate_idx, bm, bn)]
    outs = [_tile(jax.ShapeDtypeStruct((t, d), jnp.float32), bm, bn)]
    epi = _glu_residual_epilogue if glu else _residual_epilogue
    return _matmul_wres(name, lhs, w, layer, [0, nn] if glu else [0], extras, outs, epi,
                        bm=bm, bn=bn, nn=nn, kblk=kblk, kidx=kidx,
                        epilogue_bytes=2 * _nbytes((bm, bn), jnp.float32))[0]


def _mlp(lay, h, w1, w2, layer, x_res, mod):
    t, d = x_res.shape
    dff = w1.shape[2]
    bm = lay.bm
    bn = _pick(dff, 512)
    a = _matmul_wres("mlp_up", h, w1, layer, [0], [],
                     [_tile(jax.ShapeDtypeStruct((t, dff), jnp.bfloat16), bm, bn)],
                     _relu2_epilogue, bm=bm, bn=bn, nn=dff // bn,
                     epilogue_bytes=_nbytes((bm, bn), jnp.float32))[0]
    kblk = _pick(dff, d)
    for c in range(dff // kblk):
        x_res = _gated_residual_wres("mlp_down", lay, a, w2, layer, x_res, mod, 5,
                                     kblk=kblk, kidx=c)
    return x_res


def _cmul(ar, ai, br, bi):
    return ar * br - ai * bi, ar * bi + ai * br


def _s5_prepare(a_re, a_im, log_dt, b_re, b_im, c_re, c_im, d_skip):
    f32 = jnp.float32
    L = S5_CHUNK
    a_re = jnp.minimum(a_re.astype(f32), -1e-4)
    a_im = a_im.astype(f32)
    dt = jnp.exp(log_dt.astype(f32))[..., None]
    mag = jnp.exp(a_re * dt)
    ab_re = mag * jnp.cos(a_im * dt)
    ab_im = mag * jnp.sin(a_im * dt)
    den = a_re * a_re + a_im * a_im
    n_re = ab_re - 1.0
    q_re = (n_re * a_re + ab_im * a_im) / den
    q_im = (ab_im * a_re - n_re * a_im) / den
    b_re, b_im = b_re.astype(f32), b_im.astype(f32)
    bb_re = q_re[..., None] * b_re - q_im[..., None] * b_im
    bb_im = q_re[..., None] * b_im + q_im[..., None] * b_re
    c_re, c_im = c_re.astype(f32), c_im.astype(f32)
    bb_re, bb_im, c_re, c_im = [v.astype(jnp.bfloat16).astype(f32) for v in (bb_re, bb_im, c_re, c_im)]
    ngroups, nstate = ab_re.shape[1], ab_re.shape[2]
    cg = b_re.shape[-1]
    gpb = LANES // cg
    ncb = ngroups // gpb
    bf16 = jnp.bfloat16

    pr, pi = [jnp.ones_like(ab_re)], [jnp.zeros_like(ab_im)]
    for _ in range(L):
        nr, ni = _cmul(pr[-1], pi[-1], ab_re, ab_im)
        pr.append(nr)
        pi.append(ni)
    pw_re, pw_im = jnp.stack(pr), jnp.stack(pi)

    cp_re, cp_im = _cmul(c_re[None], c_im[None], pw_re[:L, :, :, None, :], pw_im[:L, :, :, None, :])
    kern = (jnp.einsum('kdgcp,dgpe->kdgce', cp_re, bb_re, precision=_HP)
            - jnp.einsum('kdgcp,dgpe->kdgce', cp_im, bb_im, precision=_HP))
    kf, kb = kern[:, 0], kern[:, 1]
    diag = jnp.eye(cg, dtype=f32)[None] * d_skip.astype(f32).reshape(ngroups, cg)[:, :, None]
    k_all = jnp.concatenate([kb[:0:-1], (kf[0] + kb[0] + diag)[None], kf[1:]], axis=0)
    s_idx = jnp.arange(L)[:, None]
    t_idx = jnp.arange(L)[None, :]
    tm = k_all[t_idx - s_idx + (L - 1)]
    tm = tm.transpose(2, 0, 4, 1, 3).astype(bf16)
    eye = jnp.eye(gpb, dtype=bf16)
    tm = tm.reshape(ncb, gpb, L, cg, L, cg)
    t_cat = (tm.transpose(0, 2, 1, 3, 4, 5)[:, :, :, :, :, None, :]
             * eye[None, None, :, None, None, :, None]).reshape(ncb, L * LANES, L * LANES)

    def win_dir(d, pows_re, pows_im):
        wr, wi = _cmul(pows_re[:, :, None, :], pows_im[:, :, None, :],
                       jnp.swapaxes(bb_re[d], 1, 2)[None], jnp.swapaxes(bb_im[d], 1, 2)[None])
        return wr, wi

    def pows(ks, d):
        return jnp.stack([pr[k][d] for k in ks]), jnp.stack([pi[k][d] for k in ks])

    wf_re, wf_im = win_dir(0, *pows(range(L - 1, -1, -1), 0))
    wb_re, wb_im = win_dir(1, *pows(range(L), 1))
    w4 = jnp.stack([wf_re, wf_im, wb_re, wb_im], axis=3).astype(bf16)
    w4 = w4.reshape(L, ncb, gpb, cg, 4, nstate)
    win_cat = (w4.transpose(1, 0, 2, 3, 4, 5)[:, :, :, :, :, None, :]
               * eye[None, None, :, None, None, :, None]).reshape(ncb, L * LANES, 4 * gpb * nstate)

    def wout_dir(d, pows_re, pows_im):
        vr, vi = _cmul(c_re[d][None], c_im[d][None], pows_re[:, :, None, :], pows_im[:, :, None, :])
        return vr, vi

    vf_re, vf_im = wout_dir(0, *pows(range(1, L + 1), 0))
    vb_re, vb_im = wout_dir(1, *pows(range(L, 0, -1), 1))
    v4 = jnp.stack([vf_re, -vf_im, vb_re, -vb_im], axis=0).astype(bf16)
    v4 = v4.reshape(4, L, ncb, gpb, cg, nstate)
    wout_cat = (v4.transpose(2, 0, 5, 1, 3, 4)[:, :, None, :, :, :, :]
                * eye[None, None, :, None, None, :, None]).reshape(ncb, 4 * gpb * nstate, L * LANES)

    a4 = jnp.stack([pw_re[L, 0], pw_im[L, 0], pw_re[L, 1], pw_im[L, 1]], axis=0)
    a4 = a4.reshape(4, ncb, gpb * nstate).transpose(1, 0, 2)
    a8 = jnp.concatenate([a4, jnp.zeros_like(a4)], axis=1)
    return t_cat, win_cat, wout_cat, a8


def _s5_kernel(hl_ref, hc_ref, wz_ref, wt_ref, wo_ref, a_ref, yl_ref, yc_ref, u_ref, z_ref,
               *, jl, jc):
    L, lanes = S5_CHUNK, hl_ref.shape[1]
    for t in range(L):
        u_ref[0:jl, t * lanes:(t + 1) * lanes] = hl_ref[pl.ds(t, jl, stride=L), :].astype(u_ref.dtype)
        u_ref[jl:jl + jc, t * lanes:(t + 1) * lanes] = hc_ref[pl.ds(t, jc, stride=L), :].astype(u_ref.dtype)
    z_ref[...] = jnp.dot(u_ref[...], wz_ref[...], preferred_element_type=jnp.float32)

    sw = z_ref.shape[1] // 4
    a = a_ref[...]
    afr, afi, abr, abi = a[0:1, :], a[1:2, :], a[2:3, :], a[3:4, :]
    rpb = SUBLANES

    def sweep(n, base, carry):
        nblk = n // rpb

        def body(i, carry):
            sfr, sfi, sbr, sbi = carry
            rf = pl.multiple_of(base + i * rpb, rpb)
            rb = pl.multiple_of(base + (nblk - 1 - i) * rpb, rpb)
            zfr = z_ref[pl.ds(rf, rpb), 0 * sw:1 * sw]
            zfi = z_ref[pl.ds(rf, rpb), 1 * sw:2 * sw]
            zbr = z_ref[pl.ds(rb, rpb), 2 * sw:3 * sw]
            zbi = z_ref[pl.ds(rb, rpb), 3 * sw:4 * sw]
            pfr, pfi, pbr, pbi = [], [], [], []
            for c in range(rpb):
                d = rpb - 1 - c
                pfr.append(sfr)
                pfi.append(sfi)
                pbr.insert(0, sbr)
                pbi.insert(0, sbi)
                sfr, sfi = (afr * sfr - afi * sfi + zfr[c:c + 1], afr * sfi + afi * sfr + zfi[c:c + 1])
                sbr, sbi = (abr * sbr - abi * sbi + zbr[d:d + 1], abr * sbi + abi * sbr + zbi[d:d + 1])
            z_ref[pl.ds(rf, rpb), 0 * sw:1 * sw] = jnp.concatenate(pfr, axis=0)
            z_ref[pl.ds(rf, rpb), 1 * sw:2 * sw] = jnp.concatenate(pfi, axis=0)
            z_ref[pl.ds(rb, rpb), 2 * sw:3 * sw] = jnp.concatenate(pbr, axis=0)
            z_ref[pl.ds(rb, rpb), 3 * sw:4 * sw] = jnp.concatenate(pbi, axis=0)
            return sfr, sfi, sbr, sbi

        return lax.fori_loop(0, nblk, body, carry)

    zero = jnp.zeros((1, sw), jnp.float32)
    carry = sweep(jc, jl, (zero, zero, zero, zero))
    sweep(jl, 0, carry)

    y = jnp.dot(u_ref[...], wt_ref[...], preferred_element_type=jnp.float32)
    y = y + jnp.dot(z_ref[...].astype(jnp.bfloat16), wo_ref[...], preferred_element_type=jnp.float32)
    y = jax.nn.gelu(y)
    for t in range(L):
        yl_ref[pl.ds(t, jl, stride=L), :] = y[0:jl, t * lanes:(t + 1) * lanes].astype(yl_ref.dtype)
        yc_ref[pl.ds(t, jc, stride=L), :] = y[jl:jl + jc, t * lanes:(t + 1) * lanes].astype(yc_ref.dtype)


def _s5_mixer(lay, h, params, out_dtype):
    t_cat, win_cat, wout_cat, a8 = params
    b, d = lay.batch, lay.d
    L = S5_CHUNK
    jl, jc = lay.seq // L, lay.ctx // L
    assert jl % SUBLANES == 0 and jc % SUBLANES == 0 and (b * jl) % jc == 0
    cw = L * LANES
    sw4 = win_cat.shape[2]
    rows = jl + jc
    ctx0 = lay.tl // lay.ctx
    once = pl.Buffered(1)
    vmem = 2 * _nbytes((rows, L, LANES), h.dtype) + 2 * _nbytes((rows, L, LANES), out_dtype) \
        + 2 * _nbytes((cw, cw), jnp.bfloat16) + _nbytes((cw, sw4), jnp.bfloat16) \
        + _nbytes((rows, cw), jnp.bfloat16) + 4 * _nbytes((rows, max(cw, sw4)), jnp.float32)
    yl, yc = pl.pallas_call(
        functools.partial(_s5_kernel, jl=jl, jc=jc),
        grid=(d // LANES, b),
        in_specs=[
            pl.BlockSpec((lay.seq, LANES), lambda cb, bi: (bi, cb)),
            pl.BlockSpec((lay.ctx, LANES), lambda cb, bi: (ctx0 + bi, cb)),
            pl.BlockSpec((None, cw, sw4), lambda cb, bi: (cb, 0, 0), pipeline_mode=once),
            pl.BlockSpec((None, cw, cw), lambda cb, bi: (cb, 0, 0), pipeline_mode=once),
            pl.BlockSpec((None, sw4, cw), lambda cb, bi: (cb, 0, 0), pipeline_mode=once),
            pl.BlockSpec((None, SUBLANES, a8.shape[2]), lambda cb, bi: (cb, 0, 0)),
        ],
        out_specs=[pl.BlockSpec((lay.seq, LANES), lambda cb, bi: (bi, cb)),
                   pl.BlockSpec((lay.ctx, LANES), lambda cb, bi: (bi, cb))],
        out_shape=[jax.ShapeDtypeStruct((lay.tl, d), out_dtype),
                   jax.ShapeDtypeStruct((lay.tc, d), out_dtype)],
        scratch_shapes=[pltpu.VMEM((rows, cw), jnp.bfloat16), pltpu.VMEM((rows, sw4), jnp.float32)],
        compiler_params=_cparams(("arbitrary", "arbitrary"), vmem),
        name="s5_chunked",
    )(h, h, win_cat, t_cat, wout_cat, a8)
    return jnp.concatenate([yl, yc], axis=0).astype(jnp.bfloat16)


def _rope_tables(lay, bm):
    n = lay.seq
    rows = n // GRID_W
    row = jnp.broadcast_to(jnp.arange(rows)[:, None], (rows, GRID_W)).reshape(-1)
    col = jnp.broadcast_to(jnp.arange(GRID_W)[None, :], (rows, GRID_W)).reshape(-1)
    n_freq = HEAD_DIM // 4
    inv_freq = ROPE_THETA ** (-jnp.arange(n_freq, dtype=jnp.float32) / n_freq)
    ang_r = row.astype(jnp.float32)[:, None] * inv_freq
    ang_c = col.astype(jnp.float32)[:, None] * inv_freq
    cr, sr, cc, sc = jnp.cos(ang_r), jnp.sin(ang_r), jnp.cos(ang_c), jnp.sin(ang_c)
    cos = jnp.concatenate([cr, cr, cc, cc], axis=-1)
    sin = jnp.concatenate([-sr, sr, -sc, sc], axis=-1)
    cos = jnp.concatenate([cos, jnp.ones((bm, HEAD_DIM), jnp.float32)], axis=0)
    sin = jnp.concatenate([sin, jnp.zeros((bm, HEAD_DIM), jnp.float32)], axis=0)
    return cos, sin


def _norm_rope(x, gain, cos, sin, scale):
    ms = jnp.mean(x * x, axis=-1, keepdims=True)
    xn = x * lax.rsqrt(ms + EPS) * gain
    q4 = HEAD_DIM // 4
    lane = lax.broadcasted_iota(jnp.int32, xn.shape, 1)
    partner = jnp.where(lane % (2 * q4) < q4,
                        pltpu.roll(xn, HEAD_DIM - q4, 1),
                        pltpu.roll(xn, q4, 1))
    out = xn * cos + partner * sin
    return out * scale if scale != 1.0 else out


def _q_epilogue(accs, extra_refs, out_refs, i, j, *, scale):
    gain_ref, cos_ref, sin_ref = extra_refs
    acc = accs[0]
    cos, sin, gain = cos_ref[...], sin_ref[...], gain_ref[...]
    for hh in range(acc.shape[1] // HEAD_DIM):
        sl = slice(hh * HEAD_DIM, (hh + 1) * HEAD_DIM)
        out_refs[0][:, sl] = _norm_rope(acc[:, sl], gain, cos, sin, scale).astype(out_refs[0].dtype)


def _kv_epilogue(accs, extra_refs, out_refs, i, j):
    gain_ref, cos_ref, sin_ref = extra_refs
    k_acc, v_acc = accs
    cos, sin, gain = cos_ref[...], sin_ref[...], gain_ref[...]
    for hh in range(k_acc.shape[1] // HEAD_DIM):
        sl = slice(hh * HEAD_DIM, (hh + 1) * HEAD_DIM)
        out_refs[0][:, sl] = _norm_rope(k_acc[:, sl], gain, cos, sin, 1.0).astype(out_refs[0].dtype)
    out_refs[1][...] = v_acc.astype(out_refs[1].dtype)


def _attn_kernel(q_ref, kx_ref, kc_ref, vx_ref, vc_ref, o_ref, *, with_latent):
    dn = (((1,), (1,)), ((), ()))
    for r in range(KV_REP):
        sl = slice(r * HEAD_DIM, (r + 1) * HEAD_DIM)
        q = q_ref[:, sl]
        sc = lax.dot_general(q, kc_ref[...], dn, preferred_element_type=jnp.float32)
        m = jnp.max(sc, axis=-1, keepdims=True)
        if with_latent:
            sx = lax.dot_general(q, kx_ref[...], dn, preferred_element_type=jnp.float32)
            m = jnp.maximum(m, jnp.max(sx, axis=-1, keepdims=True))
            px = jnp.exp(sx - m)
        pc = jnp.exp(sc - m)
        l = jnp.sum(pc, axis=-1, keepdims=True)
        o = jnp.dot(pc.astype(vc_ref.dtype), vc_ref[...], preferred_element_type=jnp.float32)
        if with_latent:
            l = l + jnp.sum(px, axis=-1, keepdims=True)
            o = o + jnp.dot(px.astype(vx_ref.dtype), vx_ref[...], preferred_element_type=jnp.float32)
        o_ref[:, sl] = (o / l).astype(o_ref.dtype)


def _attention(lay, q, k, v):
    t, dq = q.shape
    nkv = k.shape[1] // HEAD_DIM
    b, seq, ctx = lay.batch, lay.seq, lay.ctx
    qw = KV_REP * HEAD_DIM
    tq = _pick(seq, 256, SUBLANES)
    nq = seq // tq
    cblk0 = lay.tl // ctx
    kx_spec = pl.BlockSpec((seq, HEAD_DIM), lambda bi, g, qi: (bi, g))
    kc_spec = pl.BlockSpec((ctx, HEAD_DIM), lambda bi, g, qi: (cblk0 + bi, g))
    vmem = 4 * _nbytes((tq, qw), jnp.bfloat16) + 8 * _nbytes((seq + ctx, HEAD_DIM), jnp.bfloat16) \
        + 4 * _nbytes((tq, seq + ctx), jnp.float32)
    o_lat = pl.pallas_call(
        functools.partial(_attn_kernel, with_latent=True),
        grid=(b, nkv, nq),
        in_specs=[pl.BlockSpec((tq, qw), lambda bi, g, qi: (bi * nq + qi, g)),
                  kx_spec, kc_spec, kx_spec, kc_spec],
        out_specs=pl.BlockSpec((tq, qw), lambda bi, g, qi: (bi * nq + qi, g)),
        out_shape=jax.ShapeDtypeStruct((t, dq), jnp.bfloat16),
        compiler_params=_cparams(("parallel", "parallel", "arbitrary"), vmem),
        name="attn_latent",
    )(q, k, k, v, v)
    kc2 = pl.BlockSpec((ctx, HEAD_DIM), lambda bi, g: (cblk0 + bi, g))

    def ctx_kernel(q_ref, kc_ref, vc_ref, prev_ref, o_ref):
        del prev_ref
        _attn_kernel(q_ref, None, kc_ref, None, vc_ref, o_ref, with_latent=False)

    return pl.pallas_call(
        ctx_kernel,
        grid=(b, nkv),
        in_specs=[pl.BlockSpec((ctx, qw), lambda bi, g: (cblk0 + bi, g)), kc2, kc2,
                  pl.BlockSpec(memory_space=pl.ANY)],
        out_specs=pl.BlockSpec((ctx, qw), lambda bi, g: (cblk0 + bi, g)),
        out_shape=jax.ShapeDtypeStruct((t, dq), jnp.bfloat16),
        input_output_aliases={3: 0},
        compiler_params=_cparams(("parallel", "parallel"), 16 * 1024 * 1024),
        name="attn_context",
    )(q, k, v, o_lat)


def _attention_mixer(lay, h, w_qkv, layer, dq, dkv, q_gain, k_gain):
    t, d = h.shape
    bm = lay.bm
    cos, sin = _rope_tables(lay, bm)
    nseq = lay.seq // bm

    def pos_block(i):
        return jnp.where(i * bm < lay.tl, i % nseq, nseq)

    tab = lambda arr: (arr, (bm, HEAD_DIM), lambda i, j: (pos_block(i), 0))
    gain = lambda g: (g.reshape(1, HEAD_DIM), (1, HEAD_DIM), lambda i, j: (0, 0))
    bn = _pick(dq, 512)
    q = _matmul_wres("attn_q", h, w_qkv, layer, [0], [gain(q_gain), tab(cos), tab(sin)],
                     [_tile(jax.ShapeDtypeStruct((t, dq), jnp.bfloat16), bm, bn)],
                     functools.partial(_q_epilogue, scale=HEAD_DIM ** -0.5),
                     bm=bm, bn=bn, nn=dq // bn, epilogue_bytes=4 * _nbytes((bm, bn), jnp.float32))[0]
    bn = _pick(dkv, 256)
    o_sds = jax.ShapeDtypeStruct((t, dkv), jnp.bfloat16)
    k, v = _matmul_wres("attn_kv", h, w_qkv, layer, [dq // bn, (dq + dkv) // bn],
                        [gain(k_gain), tab(cos), tab(sin)],
                        [_tile(o_sds, bm, bn), _tile(o_sds, bm, bn)],
                        _kv_epilogue, bm=bm, bn=bn, nn=dkv // bn,
                        epilogue_bytes=4 * _nbytes((bm, bn), jnp.float32))
    return _attention(lay, q, k, v)


def _conv_in_epilogue(accs, extra_refs, out_refs, i, j):
    b_gate, c_gate, v = accs
    out_refs[0][...] = b_gate.astype(out_refs[0].dtype)
    out_refs[1][...] = (c_gate * v).astype(out_refs[1].dtype)


def _conv_kernel(b_ref, u_ref, up_ref, un_ref, w_ref, o_ref, *, rb, lay):
    i = pl.program_id(0)
    row0 = i * rb
    is_lat = row0 < lay.tl
    seq_len = jnp.where(is_lat, lay.seq, lay.ctx)
    local = jnp.where(is_lat, row0, row0 - lay.tl) % seq_len
    first = local == 0
    last = local + rb == seq_len
    u = u_ref[...].astype(jnp.float32)
    prev_row = jnp.where(first, 0.0, up_ref[SUBLANES - 1:SUBLANES, :].astype(jnp.float32))
    next_row = jnp.where(last, 0.0, un_ref[0:1, :].astype(jnp.float32))
    ridx = lax.broadcasted_iota(jnp.int32, u.shape, 0)
    u_prev = jnp.where(ridx == 0, prev_row, pltpu.roll(u, 1, 0))
    u_next = jnp.where(ridx == rb - 1, next_row, pltpu.roll(u, rb - 1, 0))
    w = w_ref[...]
    conv = w[0:1, :] * u_prev + w[1:2, :] * u + w[2:3, :] * u_next
    o_ref[...] = (b_ref[...].astype(jnp.float32) * conv).astype(o_ref.dtype)


def _short_conv(lay, b_gate, u, conv_w):
    t, d = u.shape
    rb = _pick(lay.ctx, 256, SUBLANES)
    assert lay.ctx % rb == 0 and lay.seq % rb == 0
    bc = _pick(d, 1024)
    sub = rb // SUBLANES
    nsub = t // SUBLANES
    w8 = jnp.concatenate([conv_w.astype(jnp.float32),
                          jnp.zeros((SUBLANES - conv_w.shape[0], d), jnp.float32)], axis=0)
    return pl.pallas_call(
        functools.partial(_conv_kernel, rb=rb, lay=lay),
        grid=(t // rb, d // bc),
        in_specs=[
            pl.BlockSpec((rb, bc), lambda i, j: (i, j)),
            pl.BlockSpec((rb, bc), lambda i, j: (i, j)),
            pl.BlockSpec((SUBLANES, bc), lambda i, j: (jnp.maximum(i * sub - 1, 0), j)),
            pl.BlockSpec((SUBLANES, bc), lambda i, j: (jnp.minimum((i + 1) * sub, nsub - 1), j)),
            pl.BlockSpec((SUBLANES, bc), lambda i, j: (0, j)),
        ],
        out_specs=pl.BlockSpec((rb, bc), lambda i, j: (i, j)),
        out_shape=jax.ShapeDtypeStruct((t, d), jnp.bfloat16),
        compiler_params=_cparams(("parallel", "parallel"),
                                 6 * _nbytes((rb, bc), jnp.bfloat16) + 8 * _nbytes((rb, bc), jnp.float32)),
        name="short_conv",
    )(b_gate, u, u, u, w8)


def _conv_mixer(lay, h, w_in, layer, conv_w):
    t, d = h.shape
    bm = _pick(lay.bm, 512, SUBLANES)
    bn = _pick(d, 256)
    nn = d // bn
    o_sds = jax.ShapeDtypeStruct((t, d), jnp.bfloat16)
    b_gate, u = _matmul_wres("conv_in", h, w_in, layer, [0, nn, 2 * nn], [],
                             [_tile(o_sds, bm, bn), _tile(o_sds, bm, bn)],
                             _conv_in_epilogue, bm=bm, bn=bn, nn=nn,
                             epilogue_bytes=_nbytes((bm, bn), jnp.float32))
    return _short_conv(lay, b_gate, u, conv_w)


def kernel(x, c, ctx, c_ctx, ada_down, ada_up, ada_bias, norm1_g, norm2_g, s5_a_re, s5_a_im, s5_log_dt, s5_b_re, s5_b_im, s5_c_re, s5_c_im, s5_d, s5_w_glu, attn_w_qkv, attn_q_gain, attn_k_gain, attn_w_o, conv_w_in, conv_w, conv_w_out, mlp_w1, mlp_w2):
    batch, seq, d = x.shape
    n_ctx = ctx.shape[1]
    depth = ada_down.shape[0]
    lay = _Layout(batch, seq, n_ctx, d)
    bf16 = jnp.bfloat16

    xs = jnp.concatenate([x.reshape(lay.tl, d), ctx.reshape(lay.tc, d)], axis=0)

    cond = jnp.concatenate([c, c_ctx[None, :],
                            jnp.zeros((MOD_ROWS - batch - 1, d), c.dtype)], axis=0)
    mods = _ada_all(cond, ada_down, ada_up, ada_bias)
    mods = mods.reshape(depth, MOD_ROWS, N_MOD, 1, d).transpose(0, 2, 1, 3, 4)

    n_mixers = 3
    for i in range(depth):
        kind, j = i % n_mixers, i // n_mixers
        mod = mods[i]
        h = _norm_mod(lay, xs, norm1_g[i], mod, 0, 1,
                      out_dtype=jnp.float32 if kind == 0 else bf16)
        if kind == 0:
            params = _s5_prepare(s5_a_re[j], s5_a_im[j], s5_log_dt[j], s5_b_re[j], s5_b_im[j],
                                 s5_c_re[j], s5_c_im[j], s5_d[j])
            y = _s5_mixer(lay, h, params, jnp.float32)
            xs = _gated_residual_wres("s5_glu", lay, y, s5_w_glu, j, xs, mod, 2, glu=True)
        elif kind == 1:
            dq = attn_w_o.shape[1]
            dkv = (attn_w_qkv.shape[2] - dq) // 2
            o = _attention_mixer(lay, h, attn_w_qkv, j, dq, dkv, attn_q_gain[j], attn_k_gain[j])
            xs = _gated_residual_wres("attn_out", lay, o, attn_w_o, j, xs, mod, 2)
        else:
            bu = _conv_mixer(lay, h, conv_w_in, j, conv_w[j])
            xs = _gated_residual_wres("conv_out", lay, bu, conv_w_out, j, xs, mod, 2)
        h = _norm_mod(lay, xs, norm2_g[i], mod, 3, 4)
        xs = _mlp(lay, h, mlp_w1, mlp_w2, i, xs, mod)
    return xs[:lay.tl].reshape(batch, seq, d)
```

```python
import functools

import jax
import jax.numpy as jnp
from jax import lax
from jax.experimental import pallas as pl
from jax.experimental.pallas import tpu as pltpu

EPS = 1e-6
N_MOD = 6
S5_GROUP = 16
S5_CHUNK = 16
HEAD_DIM = 128
KV_REP = 4
GRID_W = 64
ROPE_THETA = 10000.0
LANES = 128
SUBLANES = 8
V7X_VMEM_BYTES = 64 * 1024 * 1024
VMEM_CEILING = V7X_VMEM_BYTES - 4 * 1024 * 1024
MOD_ROWS = 8

_HP = lax.Precision.HIGHEST


INTERNAL_SCRATCH_BYTES = 2 * 1024 * 1024


def _cparams(sem, vmem_bytes):
    vmem_bytes += INTERNAL_SCRATCH_BYTES
    return pltpu.CompilerParams(
        dimension_semantics=sem,
        vmem_limit_bytes=int(min(max(vmem_bytes, 16 * 1024 * 1024), VMEM_CEILING)))


def _nbytes(shape, dtype):
    n = 1
    for s in shape:
        if s is not None:
            n *= s
    return n * jnp.dtype(dtype).itemsize


def _pick(n, target, mult=LANES):
    best = None
    for d in range(mult, min(n, target) + 1, mult):
        if n % d == 0:
            best = d
    return best if best is not None else n


def _ada_kernel(cond_ref, down_ref, up_ref, b_ref, o_ref, t_ref):
    @pl.when(pl.program_id(1) == 0)
    def _():
        cnd = cond_ref[...]
        s = cnd * jax.nn.sigmoid(cnd)
        t_ref[...] = jnp.dot(s.astype(jnp.bfloat16), down_ref[...].astype(jnp.bfloat16),
                             preferred_element_type=jnp.float32)

    o_ref[...] = jnp.dot(t_ref[...].astype(jnp.bfloat16), up_ref[...].astype(jnp.bfloat16),
                         preferred_element_type=jnp.float32) + b_ref[...]


def _ada_all(cond, ada_down, ada_up, ada_bias):
    depth, d, r = ada_down.shape
    n = ada_up.shape[2]
    bn = _pick(n, 2048)
    vmem = 2 * (_nbytes((MOD_ROWS, d), jnp.float32) + _nbytes((d, r), jnp.float32)
                + _nbytes((r, bn), jnp.float32) + 2 * _nbytes((MOD_ROWS, bn), jnp.float32))
    return pl.pallas_call(
        _ada_kernel,
        grid=(depth, n // bn),
        in_specs=[
            pl.BlockSpec((MOD_ROWS, d), lambda l, j: (0, 0)),
            pl.BlockSpec((None, d, r), lambda l, j: (l, 0, 0)),
            pl.BlockSpec((None, r, bn), lambda l, j: (l, 0, j)),
            pl.BlockSpec((None, 1, bn), lambda l, j: (l, 0, j)),
        ],
        out_specs=pl.BlockSpec((None, MOD_ROWS, bn), lambda l, j: (l, 0, j)),
        out_shape=jax.ShapeDtypeStruct((depth, MOD_ROWS, n), jnp.float32),
        scratch_shapes=[pltpu.VMEM((MOD_ROWS, r), jnp.float32)],
        compiler_params=_cparams(("arbitrary", "arbitrary"), 2 * vmem),
        name="ada_mod",
    )(cond, ada_down, ada_up, ada_bias.reshape(depth, 1, n))


def _norm_kernel(x_ref, g_ref, sh_ref, sc_ref, o_ref):
    x = x_ref[...]
    ms = jnp.mean(x * x, axis=-1, keepdims=True)
    y = x * lax.rsqrt(ms + EPS) * g_ref[...]
    o_ref[...] = (y * (1.0 + sc_ref[...]) + sh_ref[...]).astype(o_ref.dtype)


class _Layout:
    def __init__(self, batch, seq, ctx, d):
        self.batch, self.seq, self.ctx, self.d = batch, seq, ctx, d
        self.tl, self.tc = batch * seq, batch * ctx
        self.t = self.tl + self.tc
        self.bm = _pick(self.tc, 1024, SUBLANES)
        assert seq % self.bm == 0 and self.tc % self.bm == 0
        assert batch + 1 <= MOD_ROWS

    def mod_row(self, i, bm):
        return jnp.where(i * bm < self.tl, (i * bm) // self.seq, self.batch)


def _norm_mod(lay, x, g, mod, shift_idx, scale_idx, out_dtype=jnp.bfloat16):
    t, d = x.shape
    bm = _pick(lay.bm, 256, SUBLANES)
    vmem = 2 * (_nbytes((bm, d), jnp.float32) + _nbytes((bm, d), out_dtype)) \
        + 4 * _nbytes((bm, d), jnp.float32)

    def mod_spec(which):
        return pl.BlockSpec((None, None, 1, d), lambda i: (which, lay.mod_row(i, bm), 0, 0))

    return pl.pallas_call(
        _norm_kernel,
        grid=(t // bm,),
        in_specs=[
            pl.BlockSpec((bm, d), lambda i: (i, 0)),
            pl.BlockSpec((1, d), lambda i: (0, 0)),
            mod_spec(shift_idx),
            mod_spec(scale_idx),
        ],
        out_specs=pl.BlockSpec((bm, d), lambda i: (i, 0)),
        out_shape=jax.ShapeDtypeStruct((t, d), out_dtype),
        compiler_params=_cparams(("parallel",), vmem),
        name="norm_mod",
    )(x, g.reshape(1, d), mod, mod)


def _tile(arr, bm, bn):
    return (arr, (bm, bn), lambda i, j: (i, j))


def _mod_op(lay, mod, which, bm, bn):
    return (mod, (None, None, 1, bn), lambda i, j: (which, lay.mod_row(i, bm), 0, j))


def _mm_wres_kernel(*refs, n_rhs, n_extra, n_out, epilogue):
    lhs_ref = refs[0]
    w_refs = refs[1:1 + n_rhs]
    extra_refs = refs[1 + n_rhs:1 + n_rhs + n_extra]
    out_refs = refs[1 + n_rhs + n_extra:1 + n_rhs + n_extra + n_out]
    wbf_refs = refs[1 + n_rhs + n_extra + n_out:]
    j, i = pl.program_id(0), pl.program_id(1)

    @pl.when(i == 0)
    def _():
        for wb, w in zip(wbf_refs, w_refs):
            wb[...] = w[...].astype(wb.dtype)

    accs = [jnp.dot(lhs_ref[...], wb[...], preferred_element_type=jnp.float32) for wb in wbf_refs]
    epilogue(accs, extra_refs, out_refs, i, j)


def _matmul_wres(name, lhs, w, layer, offs, extras, outs, epilogue, *, bm, bn, nn,
                 kblk=None, kidx=0, epilogue_bytes=0):
    m = lhs.shape[0]
    kdim = lhs.shape[1] if kblk is None else kblk
    assert m % bm == 0 and w.shape[1] == lhs.shape[1] and lhs.shape[1] % kdim == 0

    def bind(shape, fn):
        return pl.BlockSpec(shape, lambda j, i: fn(i, j))

    in_specs = [pl.BlockSpec((bm, kdim), lambda j, i: (i, kidx))]
    for off in offs:
        in_specs.append(pl.BlockSpec((None, kdim, bn), lambda j, i, off=off: (layer, kidx, off + j)))
    in_specs += [bind(s, f) for _, s, f in extras]
    vmem = 2 * _nbytes((bm, kdim), lhs.dtype)
    vmem += len(offs) * (2 * _nbytes((kdim, bn), w.dtype) + _nbytes((kdim, bn), jnp.bfloat16))
    vmem += sum(2 * _nbytes(s, a.dtype) for a, s, _ in extras + outs)
    vmem += 2 * len(offs) * _nbytes((bm, bn), jnp.float32) + epilogue_bytes
    kern = functools.partial(_mm_wres_kernel, n_rhs=len(offs), n_extra=len(extras),
                             n_out=len(outs), epilogue=epilogue)
    return pl.pallas_call(
        kern,
        grid=(nn, m // bm),
        in_specs=in_specs,
        out_specs=[bind(s, f) for _, s, f in outs],
        out_shape=[a for a, _, _ in outs],
        scratch_shapes=[pltpu.VMEM((kdim, bn), jnp.bfloat16) for _ in offs],
        compiler_params=_cparams(("arbitrary", "arbitrary"), vmem),
        name=name,
    )(lhs, *[w for _ in offs], *[a for a, _, _ in extras])


def _residual_epilogue(accs, extra_refs, out_refs, i, j):
    res_ref, gate_ref = extra_refs
    out_refs[0][...] = res_ref[...] + gate_ref[...] * accs[0]


def _glu_residual_epilogue(accs, extra_refs, out_refs, i, j):
    res_ref, gate_ref = extra_refs
    a, g = accs
    out_refs[0][...] = res_ref[...] + gate_ref[...] * (a * jax.nn.sigmoid(g))


def _relu2_epilogue(accs, extra_refs, out_refs, i, j):
    r = jnp.maximum(accs[0], 0.0)
    out_refs[0][...] = (r * r).astype(out_refs[0].dtype)


def _gated_residual_wres(name, lay, lhs, w, layer, x_res, mod, gate_idx, *, glu=False,
                         kblk=None, kidx=0):
    t, d = x_res.shape
    bm = lay.bm
    bn = _pick(d, 256 if glu else 512)
    nn = d // bn
    extras = [_tile(x_res, bm, bn), _mod_op(lay, mod, gate_idx, bm, bn)]
    outs = [_tile(jax.ShapeDtypeStruct((t, d), jnp.float32), bm, bn)]
    epi = _glu_residual_epilogue if glu else _residual_epilogue
    return _matmul_wres(name, lhs, w, layer, [0, nn] if glu else [0], extras, outs, epi,
                        bm=bm, bn=bn, nn=nn, kblk=kblk, kidx=kidx,
                        epilogue_bytes=2 * _nbytes((bm, bn), jnp.float32))[0]


def _mlp(lay, h, w1, w2, layer, x_res, mod):
    t, d = x_res.shape
    dff = w1.shape[2]
    bm = lay.bm
    bn = _pick(dff, 512)
    a = _matmul_wres("mlp_up", h, w1, layer, [0], [],
                     [_tile(jax.ShapeDtypeStruct((t, dff), jnp.bfloat16), bm, bn)],
                     _relu2_epilogue, bm=bm, bn=bn, nn=dff // bn,
                     epilogue_bytes=_nbytes((bm, bn), jnp.float32))[0]
    kblk = _pick(dff, d)
    for c in range(dff // kblk):
        x_res = _gated_residual_wres("mlp_down", lay, a, w2, layer, x_res, mod, 5,
                                     kblk=kblk, kidx=c)
    return x_res


def _cmul(ar, ai, br, bi):
    return ar * br - ai * bi, ar * bi + ai * br


def _expand_groups(compact, outer, inner, gpb, row_group):
    k = outer * inner
    col = jnp.arange(outer * gpb * inner)
    src = (col // (gpb * inner)) * inner + col % inner
    spread = (jnp.arange(k)[:, None] == src[None, :]).astype(compact.dtype)
    keep = row_group[:, None] == ((col // inner) % gpb)[None, :]
    full = jnp.einsum('brk,kn->brn', compact, spread, preferred_element_type=jnp.float32)
    return jnp.where(keep[None], full, 0.0).astype(compact.dtype)


def _s5_prepare(a_re, a_im, log_dt, b_re, b_im, c_re, c_im, d_skip):
    f32 = jnp.float32
    L = S5_CHUNK
    a_re = jnp.minimum(a_re.astype(f32), -1e-4)
    a_im = a_im.astype(f32)
    dt = jnp.exp(log_dt.astype(f32))[..., None]
    mag = jnp.exp(a_re * dt)
    ab_re = mag * jnp.cos(a_im * dt)
    ab_im = mag * jnp.sin(a_im * dt)
    den = a_re * a_re + a_im * a_im
    n_re = ab_re - 1.0
    q_re = (n_re * a_re + ab_im * a_im) / den
    q_im = (ab_im * a_re - n_re * a_im) / den
    b_re, b_im = b_re.astype(f32), b_im.astype(f32)
    bb_re = q_re[..., None] * b_re - q_im[..., None] * b_im
    bb_im = q_re[..., None] * b_im + q_im[..., None] * b_re
    c_re, c_im = c_re.astype(f32), c_im.astype(f32)
    bb_re, bb_im, c_re, c_im = [v.astype(jnp.bfloat16).astype(f32) for v in (bb_re, bb_im, c_re, c_im)]
    ngroups, nstate = ab_re.shape[1], ab_re.shape[2]
    cg = b_re.shape[-1]
    gpb = LANES // cg
    ncb = ngroups // gpb
    bf16 = jnp.bfloat16

    pr, pi = [jnp.ones_like(ab_re)], [jnp.zeros_like(ab_im)]
    for _ in range(L):
        nr, ni = _cmul(pr[-1], pi[-1], ab_re, ab_im)
        pr.append(nr)
        pi.append(ni)
    pw_re, pw_im = jnp.stack(pr), jnp.stack(pi)

    cp_re, cp_im = _cmul(c_re[None], c_im[None], pw_re[:L, :, :, None, :], pw_im[:L, :, :, None, :])
    kern = (jnp.einsum('kdgcp,dgpe->kdgce', cp_re, bb_re, precision=_HP)
            - jnp.einsum('kdgcp,dgpe->kdgce', cp_im, bb_im, precision=_HP))
    kf, kb = kern[:, 0], kern[:, 1]
    diag = jnp.eye(cg, dtype=f32)[None] * d_skip.astype(f32).reshape(ngroups, cg)[:, :, None]
    k_all = jnp.concatenate([kb[:0:-1], (kf[0] + kb[0] + diag)[None], kf[1:]], axis=0)
    s_idx = jnp.arange(L)[:, None]
    t_idx = jnp.arange(L)[None, :]
    tm = k_all[t_idx - s_idx + (L - 1)]
    tm = tm.transpose(2, 0, 4, 1, 3).astype(bf16)
    token_rows = (jnp.arange(L * LANES) // cg) % gpb
    state_rows = (jnp.arange(4 * gpb * nstate) // nstate) % gpb
    tc = tm.reshape(ncb, gpb, L, cg, L * cg).transpose(0, 2, 1, 3, 4).reshape(ncb, L * LANES, L * cg)
    t_cat = _expand_groups(tc, L, cg, gpb, token_rows)

    def win_dir(d, pows_re, pows_im):
        wr, wi = _cmul(pows_re[:, :, None, :], pows_im[:, :, None, :],
                       jnp.swapaxes(bb_re[d], 1, 2)[None], jnp.swapaxes(bb_im[d], 1, 2)[None])
        return wr, wi

    def pows(ks, d):
        return jnp.stack([pr[k][d] for k in ks]), jnp.stack([pi[k][d] for k in ks])

    wf_re, wf_im = win_dir(0, *pows(range(L - 1, -1, -1), 0))
    wb_re, wb_im = win_dir(1, *pows(range(L), 1))
    w4 = jnp.stack([wf_re, wf_im, wb_re, wb_im], axis=3).astype(bf16)
    wc = w4.reshape(L, ncb, gpb, cg, 4 * nstate).transpose(1, 0, 2, 3, 4)
    win_cat = _expand_groups(wc.reshape(ncb, L * LANES, 4 * nstate), 4, nstate, gpb, token_rows)

    def wout_dir(d, pows_re, pows_im):
        vr, vi = _cmul(c_re[d][None], c_im[d][None], pows_re[:, :, None, :], pows_im[:, :, None, :])
        return vr, vi

    vf_re, vf_im = wout_dir(0, *pows(range(1, L + 1), 0))
    vb_re, vb_im = wout_dir(1, *pows(range(L, 0, -1), 1))
    v4 = jnp.stack([vf_re, -vf_im, vb_re, -vb_im], axis=0).astype(bf16)
    vc = v4.reshape(4, L, ncb, gpb, cg, nstate).transpose(2, 0, 3, 5, 1, 4)
    wout_cat = _expand_groups(vc.reshape(ncb, 4 * gpb * nstate, L * cg), L, cg, gpb, state_rows)

    a4 = jnp.stack([pw_re[L, 0], pw_im[L, 0], pw_re[L, 1], pw_im[L, 1]], axis=0)
    a4 = a4.reshape(4, ncb, gpb * nstate).transpose(1, 0, 2)
    a8 = jnp.concatenate([a4, jnp.zeros_like(a4)], axis=1)
    return t_cat, win_cat, wout_cat, a8


def _s5_kernel(hl_ref, hc_ref, wz_ref, wt_ref, wo_ref, a_ref, yl_ref, yc_ref, u_ref, z_ref,
               *, jl, jc):
    L, lanes = S5_CHUNK, hl_ref.shape[1]
    for t in range(L):
        u_ref[0:jl, t * lanes:(t + 1) * lanes] = hl_ref[pl.ds(t, jl, stride=L), :].astype(u_ref.dtype)
        u_ref[jl:jl + jc, t * lanes:(t + 1) * lanes] = hc_ref[pl.ds(t, jc, stride=L), :].astype(u_ref.dtype)
    z_ref[...] = jnp.dot(u_ref[...], wz_ref[...], preferred_element_type=jnp.float32)

    sw = z_ref.shape[1] // 4
    a = a_ref[...]
    afr, afi, abr, abi = a[0:1, :], a[1:2, :], a[2:3, :], a[3:4, :]
    rpb = SUBLANES

    def sweep(n, base, carry):
        nblk = n // rpb

        def body(i, carry):
            sfr, sfi, sbr, sbi = carry
            rf = pl.multiple_of(base + i * rpb, rpb)
            rb = pl.multiple_of(base + (nblk - 1 - i) * rpb, rpb)
            zfr = z_ref[pl.ds(rf, rpb), 0 * sw:1 * sw]
            zfi = z_ref[pl.ds(rf, rpb), 1 * sw:2 * sw]
            zbr = z_ref[pl.ds(rb, rpb), 2 * sw:3 * sw]
            zbi = z_ref[pl.ds(rb, rpb), 3 * sw:4 * sw]
            pfr, pfi, pbr, pbi = [], [], [], []
            for c in range(rpb):
                d = rpb - 1 - c
                pfr.append(sfr)
                pfi.append(sfi)
                pbr.insert(0, sbr)
                pbi.insert(0, sbi)
                sfr, sfi = (afr * sfr - afi * sfi + zfr[c:c + 1], afr * sfi + afi * sfr + zfi[c:c + 1])
                sbr, sbi = (abr * sbr - abi * sbi + zbr[d:d + 1], abr * sbi + abi * sbr + zbi[d:d + 1])
            z_ref[pl.ds(rf, rpb), 0 * sw:1 * sw] = jnp.concatenate(pfr, axis=0)
            z_ref[pl.ds(rf, rpb), 1 * sw:2 * sw] = jnp.concatenate(pfi, axis=0)
            z_ref[pl.ds(rb, rpb), 2 * sw:3 * sw] = jnp.concatenate(pbr, axis=0)
            z_ref[pl.ds(rb, rpb), 3 * sw:4 * sw] = jnp.concatenate(pbi, axis=0)
            return sfr, sfi, sbr, sbi

        return lax.fori_loop(0, nblk, body, carry)

    zero = jnp.zeros((1, sw), jnp.float32)
    carry = sweep(jc, jl, (zero, zero, zero, zero))
    sweep(jl, 0, carry)

    y = jnp.dot(u_ref[...], wt_ref[...], preferred_element_type=jnp.float32)
    y = y + jnp.dot(z_ref[...].astype(jnp.bfloat16), wo_ref[...], preferred_element_type=jnp.float32)
    y = jax.nn.gelu(y)
    for t in range(L):
        yl_ref[pl.ds(t, jl, stride=L), :] = y[0:jl, t * lanes:(t + 1) * lanes].astype(yl_ref.dtype)
        yc_ref[pl.ds(t, jc, stride=L), :] = y[jl:jl + jc, t * lanes:(t + 1) * lanes].astype(yc_ref.dtype)


def _s5_mixer(lay, h, params, out_dtype):
    t_cat, win_cat, wout_cat, a8 = params
    b, d = lay.batch, lay.d
    L = S5_CHUNK
    jl, jc = lay.seq // L, lay.ctx // L
    assert jl % SUBLANES == 0 and jc % SUBLANES == 0 and (b * jl) % jc == 0
    cw = L * LANES
    sw4 = win_cat.shape[2]
    rows = jl + jc
    ctx0 = lay.tl // lay.ctx
    once = pl.Buffered(1)
    vmem = 2 * _nbytes((rows, L, LANES), h.dtype) + 2 * _nbytes((rows, L, LANES), out_dtype) \
        + 2 * _nbytes((cw, cw), jnp.bfloat16) + _nbytes((cw, sw4), jnp.bfloat16) \
        + _nbytes((rows, cw), jnp.bfloat16) + 4 * _nbytes((rows, max(cw, sw4)), jnp.float32)
    yl, yc = pl.pallas_call(
        functools.partial(_s5_kernel, jl=jl, jc=jc),
        grid=(d // LANES, b),
        in_specs=[
            pl.BlockSpec((lay.seq, LANES), lambda cb, bi: (bi, cb)),
            pl.BlockSpec((lay.ctx, LANES), lambda cb, bi: (ctx0 + bi, cb)),
            pl.BlockSpec((None, cw, sw4), lambda cb, bi: (cb, 0, 0), pipeline_mode=once),
            pl.BlockSpec((None, cw, cw), lambda cb, bi: (cb, 0, 0), pipeline_mode=once),
            pl.BlockSpec((None, sw4, cw), lambda cb, bi: (cb, 0, 0), pipeline_mode=once),
            pl.BlockSpec((None, SUBLANES, a8.shape[2]), lambda cb, bi: (cb, 0, 0)),
        ],
        out_specs=[pl.BlockSpec((lay.seq, LANES), lambda cb, bi: (bi, cb)),
                   pl.BlockSpec((lay.ctx, LANES), lambda cb, bi: (bi, cb))],
        out_shape=[jax.ShapeDtypeStruct((lay.tl, d), out_dtype),
                   jax.ShapeDtypeStruct((lay.tc, d), out_dtype)],
        scratch_shapes=[pltpu.VMEM((rows, cw), jnp.bfloat16), pltpu.VMEM((rows, sw4), jnp.float32)],
        compiler_params=_cparams(("arbitrary", "arbitrary"), vmem),
        name="s5_chunked",
    )(h, h, win_cat, t_cat, wout_cat, a8)
    return jnp.concatenate([yl, yc], axis=0).astype(jnp.bfloat16)


def _rope_tables(lay, bm):
    n = lay.seq
    rows = n // GRID_W
    row = jnp.broadcast_to(jnp.arange(rows)[:, None], (rows, GRID_W)).reshape(-1)
    col = jnp.broadcast_to(jnp.arange(GRID_W)[None, :], (rows, GRID_W)).reshape(-1)
    n_freq = HEAD_DIM // 4
    inv_freq = ROPE_THETA ** (-jnp.arange(n_freq, dtype=jnp.float32) / n_freq)
    ang_r = row.astype(jnp.float32)[:, None] * inv_freq
    ang_c = col.astype(jnp.float32)[:, None] * inv_freq
    cr, sr, cc, sc = jnp.cos(ang_r), jnp.sin(ang_r), jnp.cos(ang_c), jnp.sin(ang_c)
    cos = jnp.concatenate([cr, cr, cc, cc], axis=-1)
    sin = jnp.concatenate([-sr, sr, -sc, sc], axis=-1)
    cos = jnp.concatenate([cos, jnp.ones((bm, HEAD_DIM), jnp.float32)], axis=0)
    sin = jnp.concatenate([sin, jnp.zeros((bm, HEAD_DIM), jnp.float32)], axis=0)
    return cos, sin


def _norm_rope(x, gain, cos, sin, scale):
    ms = jnp.mean(x * x, axis=-1, keepdims=True)
    xn = x * lax.rsqrt(ms + EPS) * gain
    q4 = HEAD_DIM // 4
    lane = lax.broadcasted_iota(jnp.int32, xn.shape, 1)
    partner = jnp.where(lane % (2 * q4) < q4,
                        pltpu.roll(xn, HEAD_DIM - q4, 1),
                        pltpu.roll(xn, q4, 1))
    out = xn * cos + partner * sin
    return out * scale if scale != 1.0 else out


def _q_epilogue(accs, extra_refs, out_refs, i, j, *, scale):
    gain_ref, cos_ref, sin_ref = extra_refs
    acc = accs[0]
    cos, sin, gain = cos_ref[...], sin_ref[...], gain_ref[...]
    for hh in range(acc.shape[1] // HEAD_DIM):
        sl = slice(hh * HEAD_DIM, (hh + 1) * HEAD_DIM)
        out_refs[0][:, sl] = _norm_rope(acc[:, sl], gain, cos, sin, scale).astype(out_refs[0].dtype)


def _kv_epilogue(accs, extra_refs, out_refs, i, j):
    gain_ref, cos_ref, sin_ref = extra_refs
    k_acc, v_acc = accs
    cos, sin, gain = cos_ref[...], sin_ref[...], gain_ref[...]
    for hh in range(k_acc.shape[1] // HEAD_DIM):
        sl = slice(hh * HEAD_DIM, (hh + 1) * HEAD_DIM)
        out_refs[0][:, sl] = _norm_rope(k_acc[:, sl], gain, cos, sin, 1.0).astype(out_refs[0].dtype)
    out_refs[1][...] = v_acc.astype(out_refs[1].dtype)


def _attn_kernel(q_ref, kx_ref, kc_ref, vx_ref, vc_ref, o_ref, *, with_latent):
    dn = (((1,), (1,)), ((), ()))
    for r in range(KV_REP):
        sl = slice(r * HEAD_DIM, (r + 1) * HEAD_DIM)
        q = q_ref[:, sl]
        sc = lax.dot_general(q, kc_ref[...], dn, preferred_element_type=jnp.float32)
        m = jnp.max(sc, axis=-1, keepdims=True)
        if with_latent:
            sx = lax.dot_general(q, kx_ref[...], dn, preferred_element_type=jnp.float32)
            m = jnp.maximum(m, jnp.max(sx, axis=-1, keepdims=True))
            px = jnp.exp(sx - m)
        pc = jnp.exp(sc - m)
        l = jnp.sum(pc, axis=-1, keepdims=True)
        o = jnp.dot(pc.astype(vc_ref.dtype), vc_ref[...], preferred_element_type=jnp.float32)
        if with_latent:
            l = l + jnp.sum(px, axis=-1, keepdims=True)
            o = o + jnp.dot(px.astype(vx_ref.dtype), vx_ref[...], preferred_element_type=jnp.float32)
        o_ref[:, sl] = (o / l).astype(o_ref.dtype)


def _attention(lay, q, k, v):
    t, dq = q.shape
    nkv = k.shape[1] // HEAD_DIM
    b, seq, ctx = lay.batch, lay.seq, lay.ctx
    qw = KV_REP * HEAD_DIM
    tq = _pick(seq, 256, SUBLANES)
    nq = seq // tq
    cblk0 = lay.tl // ctx
    kx_spec = pl.BlockSpec((seq, HEAD_DIM), lambda bi, g, qi: (bi, g))
    kc_spec = pl.BlockSpec((ctx, HEAD_DIM), lambda bi, g, qi: (cblk0 + bi, g))
    vmem = 4 * _nbytes((tq, qw), jnp.bfloat16) + 8 * _nbytes((seq + ctx, HEAD_DIM), jnp.bfloat16) \
        + 4 * _nbytes((tq, seq + ctx), jnp.float32)
    o_lat = pl.pallas_call(
        functools.partial(_attn_kernel, with_latent=True),
        grid=(b, nkv, nq),
        in_specs=[pl.BlockSpec((tq, qw), lambda bi, g, qi: (bi * nq + qi, g)),
                  kx_spec, kc_spec, kx_spec, kc_spec],
        out_specs=pl.BlockSpec((tq, qw), lambda bi, g, qi: (bi * nq + qi, g)),
        out_shape=jax.ShapeDtypeStruct((t, dq), jnp.bfloat16),
        compiler_params=_cparams(("parallel", "parallel", "arbitrary"), vmem),
        name="attn_latent",
    )(q, k, k, v, v)
    kc2 = pl.BlockSpec((ctx, HEAD_DIM), lambda bi, g: (cblk0 + bi, g))

    def ctx_kernel(q_ref, kc_ref, vc_ref, prev_ref, o_ref):
        del prev_ref
        _attn_kernel(q_ref, None, kc_ref, None, vc_ref, o_ref, with_latent=False)

    return pl.pallas_call(
        ctx_kernel,
        grid=(b, nkv),
        in_specs=[pl.BlockSpec((ctx, qw), lambda bi, g: (cblk0 + bi, g)), kc2, kc2,
                  pl.BlockSpec(memory_space=pl.ANY)],
        out_specs=pl.BlockSpec((ctx, qw), lambda bi, g: (cblk0 + bi, g)),
        out_shape=jax.ShapeDtypeStruct((t, dq), jnp.bfloat16),
        input_output_aliases={3: 0},
        compiler_params=_cparams(("parallel", "parallel"), 16 * 1024 * 1024),
        name="attn_context",
    )(q, k, v, o_lat)


def _attention_mixer(lay, h, w_qkv, layer, dq, dkv, q_gain, k_gain):
    t, d = h.shape
    bm = lay.bm
    cos, sin = _rope_tables(lay, bm)
    nseq = lay.seq // bm

    def pos_block(i):
        return jnp.where(i * bm < lay.tl, i % nseq, nseq)

    tab = lambda arr: (arr, (bm, HEAD_DIM), lambda i, j: (pos_block(i), 0))
    gain = lambda g: (g.reshape(1, HEAD_DIM), (1, HEAD_DIM), lambda i, j: (0, 0))
    bn = _pick(dq, 512)
    q = _matmul_wres("attn_q", h, w_qkv, layer, [0], [gain(q_gain), tab(cos), tab(sin)],
                     [_tile(jax.ShapeDtypeStruct((t, dq), jnp.bfloat16), bm, bn)],
                     functools.partial(_q_epilogue, scale=HEAD_DIM ** -0.5),
                     bm=bm, bn=bn, nn=dq // bn, epilogue_bytes=4 * _nbytes((bm, bn), jnp.float32))[0]
    bn = _pick(dkv, 256)
    o_sds = jax.ShapeDtypeStruct((t, dkv), jnp.bfloat16)
    k, v = _matmul_wres("attn_kv", h, w_qkv, layer, [dq // bn, (dq + dkv) // bn],
                        [gain(k_gain), tab(cos), tab(sin)],
                        [_tile(o_sds, bm, bn), _tile(o_sds, bm, bn)],
                        _kv_epilogue, bm=bm, bn=bn, nn=dkv // bn,
                        epilogue_bytes=4 * _nbytes((bm, bn), jnp.float32))
    return _attention(lay, q, k, v)


def _conv_in_epilogue(accs, extra_refs, out_refs, i, j):
    b_gate, c_gate, v = accs
    out_refs[0][...] = b_gate.astype(out_refs[0].dtype)
    out_refs[1][...] = (c_gate * v).astype(out_refs[1].dtype)


def _conv_kernel(b_ref, u_ref, up_ref, un_ref, w_ref, o_ref, *, rb, lay):
    i = pl.program_id(0)
    row0 = i * rb
    is_lat = row0 < lay.tl
    seq_len = jnp.where(is_lat, lay.seq, lay.ctx)
    local = jnp.where(is_lat, row0, row0 - lay.tl) % seq_len
    first = local == 0
    last = local + rb == seq_len
    u = u_ref[...].astype(jnp.float32)
    prev_row = jnp.where(first, 0.0, up_ref[SUBLANES - 1:SUBLANES, :].astype(jnp.float32))
    next_row = jnp.where(last, 0.0, un_ref[0:1, :].astype(jnp.float32))
    ridx = lax.broadcasted_iota(jnp.int32, u.shape, 0)
    u_prev = jnp.where(ridx == 0, prev_row, pltpu.roll(u, 1, 0))
    u_next = jnp.where(ridx == rb - 1, next_row, pltpu.roll(u, rb - 1, 0))
    w = w_ref[...]
    conv = w[0:1, :] * u_prev + w[1:2, :] * u + w[2:3, :] * u_next
    o_ref[...] = (b_ref[...].astype(jnp.float32) * conv).astype(o_ref.dtype)


def _short_conv(lay, b_gate, u, conv_w):
    t, d = u.shape
    rb = _pick(lay.ctx, 256, SUBLANES)
    assert lay.ctx % rb == 0 and lay.seq % rb == 0
    bc = _pick(d, 1024)
    sub = rb // SUBLANES
    nsub = t // SUBLANES
    w8 = jnp.concatenate([conv_w.astype(jnp.float32),
                          jnp.zeros((SUBLANES - conv_w.shape[0], d), jnp.float32)], axis=0)
    return pl.pallas_call(
        functools.partial(_conv_kernel, rb=rb, lay=lay),
        grid=(t // rb, d // bc),
        in_specs=[
            pl.BlockSpec((rb, bc), lambda i, j: (i, j)),
            pl.BlockSpec((rb, bc), lambda i, j: (i, j)),
            pl.BlockSpec((SUBLANES, bc), lambda i, j: (jnp.maximum(i * sub - 1, 0), j)),
            pl.BlockSpec((SUBLANES, bc), lambda i, j: (jnp.minimum((i + 1) * sub, nsub - 1), j)),
            pl.BlockSpec((SUBLANES, bc), lambda i, j: (0, j)),
        ],
        out_specs=pl.BlockSpec((rb, bc), lambda i, j: (i, j)),
        out_shape=jax.ShapeDtypeStruct((t, d), jnp.bfloat16),
        compiler_params=_cparams(("parallel", "parallel"),
                                 6 * _nbytes((rb, bc), jnp.bfloat16) + 8 * _nbytes((rb, bc), jnp.float32)),
        name="short_conv",
    )(b_gate, u, u, u, w8)


def _conv_mixer(lay, h, w_in, layer, conv_w):
    t, d = h.shape
    bm = _pick(lay.bm, 512, SUBLANES)
    bn = _pick(d, 256)
    nn = d // bn
    o_sds = jax.ShapeDtypeStruct((t, d), jnp.bfloat16)
    b_gate, u = _matmul_wres("conv_in", h, w_in, layer, [0, nn, 2 * nn], [],
                             [_tile(o_sds, bm, bn), _tile(o_sds, bm, bn)],
                             _conv_in_epilogue, bm=bm, bn=bn, nn=nn,
                             epilogue_bytes=_nbytes((bm, bn), jnp.float32))
    return _short_conv(lay, b_gate, u, conv_w)


def kernel(x, c, ctx, c_ctx, ada_down, ada_up, ada_bias, norm1_g, norm2_g, s5_a_re, s5_a_im, s5_log_dt, s5_b_re, s5_b_im, s5_c_re, s5_c_im, s5_d, s5_w_glu, attn_w_qkv, attn_q_gain, attn_k_gain, attn_w_o, conv_w_in, conv_w, conv_w_out, mlp_w1, mlp_w2):
    batch, seq, d = x.shape
    n_ctx = ctx.shape[1]
    depth = ada_down.shape[0]
    lay = _Layout(batch, seq, n_ctx, d)
    bf16 = jnp.bfloat16

    xs = jnp.concatenate([x.reshape(lay.tl, d), ctx.reshape(lay.tc, d)], axis=0)

    cond = jnp.concatenate([c, c_ctx[None, :],
                            jnp.zeros((MOD_ROWS - batch - 1, d), c.dtype)], axis=0)
    mods = _ada_all(cond, ada_down, ada_up, ada_bias)
    mods = mods.reshape(depth, MOD_ROWS, N_MOD, 1, d).transpose(0, 2, 1, 3, 4)

    n_mixers = 3
    for i in range(depth):
        kind, j = i % n_mixers, i // n_mixers
        mod = mods[i]
        h = _norm_mod(lay, xs, norm1_g[i], mod, 0, 1,
                      out_dtype=jnp.float32 if kind == 0 else bf16)
        if kind == 0:
            params = _s5_prepare(s5_a_re[j], s5_a_im[j], s5_log_dt[j], s5_b_re[j], s5_b_im[j],
                                 s5_c_re[j], s5_c_im[j], s5_d[j])
            y = _s5_mixer(lay, h, params, jnp.float32)
            xs = _gated_residual_wres("s5_glu", lay, y, s5_w_glu, j, xs, mod, 2, glu=True)
        elif kind == 1:
            dq = attn_w_o.shape[1]
            dkv = (attn_w_qkv.shape[2] - dq) // 2
            o = _attention_mixer(lay, h, attn_w_qkv, j, dq, dkv, attn_q_gain[j], attn_k_gain[j])
            xs = _gated_residual_wres("attn_out", lay, o, attn_w_o, j, xs, mod, 2)
        else:
            bu = _conv_mixer(lay, h, conv_w_in, j, conv_w[j])
            xs = _gated_residual_wres("conv_out", lay, bu, conv_w_out, j, xs, mod, 2)
        h = _norm_mod(lay, xs, norm2_g[i], mod, 3, 4)
        xs = _mlp(lay, h, mlp_w1, mlp_w2, i, xs, mod)
    return xs[:lay.tl].reshape(batch, seq, d)
```

```python
import functools

import jax
import jax.numpy as jnp
from jax import lax
from jax.experimental import pallas as pl
from jax.experimental.pallas import tpu as pltpu

EPS = 1e-6
N_MOD = 6
S5_GROUP = 16
S5_CHUNK = 16
HEAD_DIM = 128
KV_REP = 4
GRID_W = 64
ROPE_THETA = 10000.0
LANES = 128
SUBLANES = 8
V7X_VMEM_BYTES = 64 * 1024 * 1024
VMEM_CEILING = V7X_VMEM_BYTES - 4 * 1024 * 1024
MOD_ROWS = 8

_HP = lax.Precision.HIGHEST


INTERNAL_SCRATCH_BYTES = 2 * 1024 * 1024


def _cparams(sem, vmem_bytes):
    vmem_bytes += INTERNAL_SCRATCH_BYTES
    return pltpu.CompilerParams(
        dimension_semantics=sem,
        vmem_limit_bytes=int(min(max(vmem_bytes, 16 * 1024 * 1024), VMEM_CEILING)))


def _nbytes(shape, dtype):
    n = 1
    for s in shape:
        if s is not None:
            n *= s
    return n * jnp.dtype(dtype).itemsize


def _pick(n, target, mult=LANES):
    best = None
    for d in range(mult, min(n, target) + 1, mult):
        if n % d == 0:
            best = d
    return best if best is not None else n


def _ada_kernel(cond_ref, down_ref, up_ref, b_ref, o_ref, t_ref):
    @pl.when(pl.program_id(1) == 0)
    def _():
        cnd = cond_ref[...]
        s = cnd * jax.nn.sigmoid(cnd)
        t_ref[...] = jnp.dot(s.astype(jnp.bfloat16), down_ref[...].astype(jnp.bfloat16),
                             preferred_element_type=jnp.float32)

    o_ref[...] = jnp.dot(t_ref[...].astype(jnp.bfloat16), up_ref[...].astype(jnp.bfloat16),
                         preferred_element_type=jnp.float32) + b_ref[...]


def _ada_all(cond, ada_down, ada_up, ada_bias):
    depth, d, r = ada_down.shape
    n = ada_up.shape[2]
    bn = _pick(n, 2048)
    vmem = 2 * (_nbytes((MOD_ROWS, d), jnp.float32) + _nbytes((d, r), jnp.float32)
                + _nbytes((r, bn), jnp.float32) + 2 * _nbytes((MOD_ROWS, bn), jnp.float32))
    return pl.pallas_call(
        _ada_kernel,
        grid=(depth, n // bn),
        in_specs=[
            pl.BlockSpec((MOD_ROWS, d), lambda l, j: (0, 0)),
            pl.BlockSpec((None, d, r), lambda l, j: (l, 0, 0)),
            pl.BlockSpec((None, r, bn), lambda l, j: (l, 0, j)),
            pl.BlockSpec((None, 1, bn), lambda l, j: (l, 0, j)),
        ],
        out_specs=pl.BlockSpec((None, MOD_ROWS, bn), lambda l, j: (l, 0, j)),
        out_shape=jax.ShapeDtypeStruct((depth, MOD_ROWS, n), jnp.float32),
        scratch_shapes=[pltpu.VMEM((MOD_ROWS, r), jnp.float32)],
        compiler_params=_cparams(("arbitrary", "arbitrary"), 2 * vmem),
        name="ada_mod",
    )(cond, ada_down, ada_up, ada_bias.reshape(depth, 1, n))


def _norm_kernel(x_ref, g_ref, sh_ref, sc_ref, o_ref):
    x = x_ref[...]
    ms = jnp.mean(x * x, axis=-1, keepdims=True)
    y = x * lax.rsqrt(ms + EPS) * g_ref[...]
    o_ref[...] = (y * (1.0 + sc_ref[...]) + sh_ref[...]).astype(o_ref.dtype)


class _Layout:
    def __init__(self, batch, seq, ctx, d):
        self.batch, self.seq, self.ctx, self.d = batch, seq, ctx, d
        self.tl, self.tc = batch * seq, batch * ctx
        self.t = self.tl + self.tc
        self.bm = _pick(self.tc, 1024, SUBLANES)
        assert seq % self.bm == 0 and self.tc % self.bm == 0
        assert batch + 1 <= MOD_ROWS

    def mod_row(self, i, bm):
        return jnp.where(i * bm < self.tl, (i * bm) // self.seq, self.batch)


def _norm_mod(lay, x, g, mod, shift_idx, scale_idx, out_dtype=jnp.bfloat16):
    t, d = x.shape
    bm = _pick(lay.bm, 256, SUBLANES)
    vmem = 2 * (_nbytes((bm, d), jnp.float32) + _nbytes((bm, d), out_dtype)) \
        + 4 * _nbytes((bm, d), jnp.float32)

    def mod_spec(which):
        return pl.BlockSpec((None, None, 1, d), lambda i: (which, lay.mod_row(i, bm), 0, 0))

    return pl.pallas_call(
        _norm_kernel,
        grid=(t // bm,),
        in_specs=[
            pl.BlockSpec((bm, d), lambda i: (i, 0)),
            pl.BlockSpec((1, d), lambda i: (0, 0)),
            mod_spec(shift_idx),
            mod_spec(scale_idx),
        ],
        out_specs=pl.BlockSpec((bm, d), lambda i: (i, 0)),
        out_shape=jax.ShapeDtypeStruct((t, d), out_dtype),
        compiler_params=_cparams(("parallel",), vmem),
        name="norm_mod",
    )(x, g.reshape(1, d), mod, mod)


def _tile(arr, bm, bn):
    return (arr, (bm, bn), lambda i, j: (i, j))


def _mod_op(lay, mod, which, bm, bn):
    return (mod, (None, None, 1, bn), lambda i, j: (which, lay.mod_row(i, bm), 0, j))


def _mm_wres_kernel(*refs, n_rhs, n_extra, n_out, epilogue):
    lhs_ref = refs[0]
    w_refs = refs[1:1 + n_rhs]
    extra_refs = refs[1 + n_rhs:1 + n_rhs + n_extra]
    out_refs = refs[1 + n_rhs + n_extra:1 + n_rhs + n_extra + n_out]
    wbf_refs = refs[1 + n_rhs + n_extra + n_out:]
    j, i = pl.program_id(0), pl.program_id(1)

    @pl.when(i == 0)
    def _():
        for wb, w in zip(wbf_refs, w_refs):
            wb[...] = w[...].astype(wb.dtype)

    accs = [jnp.dot(lhs_ref[...], wb[...], preferred_element_type=jnp.float32) for wb in wbf_refs]
    epilogue(accs, extra_refs, out_refs, i, j)


def _matmul_wres(name, lhs, w, layer, offs, extras, outs, epilogue, *, bm, bn, nn,
                 kblk=None, kidx=0, epilogue_bytes=0):
    m = lhs.shape[0]
    kdim = lhs.shape[1] if kblk is None else kblk
    assert m % bm == 0 and w.shape[1] == lhs.shape[1] and lhs.shape[1] % kdim == 0

    def bind(shape, fn):
        return pl.BlockSpec(shape, lambda j, i: fn(i, j))

    in_specs = [pl.BlockSpec((bm, kdim), lambda j, i: (i, kidx))]
    for off in offs:
        in_specs.append(pl.BlockSpec((None, kdim, bn), lambda j, i, off=off: (layer, kidx, off + j),
                                     pipeline_mode=pl.Buffered(1)))
    in_specs += [bind(s, f) for _, s, f in extras]
    vmem = 2 * _nbytes((bm, kdim), lhs.dtype)
    vmem += len(offs) * (_nbytes((kdim, bn), w.dtype) + _nbytes((kdim, bn), jnp.bfloat16))
    vmem += sum(2 * _nbytes(s, a.dtype) for a, s, _ in extras + outs)
    vmem += 2 * len(offs) * _nbytes((bm, bn), jnp.float32) + epilogue_bytes
    kern = functools.partial(_mm_wres_kernel, n_rhs=len(offs), n_extra=len(extras),
                             n_out=len(outs), epilogue=epilogue)
    return pl.pallas_call(
        kern,
        grid=(nn, m // bm),
        in_specs=in_specs,
        out_specs=[bind(s, f) for _, s, f in outs],
        out_shape=[a for a, _, _ in outs],
        scratch_shapes=[pltpu.VMEM((kdim, bn), jnp.bfloat16) for _ in offs],
        compiler_params=_cparams(("arbitrary", "arbitrary"), vmem),
        name=name,
    )(lhs, *[w for _ in offs], *[a for a, _, _ in extras])


def _residual_epilogue(accs, extra_refs, out_refs, i, j):
    res_ref, gate_ref = extra_refs
    out_refs[0][...] = res_ref[...] + gate_ref[...] * accs[0]


def _glu_residual_epilogue(accs, extra_refs, out_refs, i, j):
    res_ref, gate_ref = extra_refs
    a, g = accs
    out_refs[0][...] = res_ref[...] + gate_ref[...] * (a * jax.nn.sigmoid(g))


def _relu2_epilogue(accs, extra_refs, out_refs, i, j):
    r = jnp.maximum(accs[0], 0.0)
    out_refs[0][...] = (r * r).astype(out_refs[0].dtype)


def _gated_residual_wres(name, lay, lhs, w, layer, x_res, mod, gate_idx, *, glu=False,
                         kblk=None, kidx=0):
    t, d = x_res.shape
    bm = lay.bm
    bn = _pick(d, 256 if glu else 512)
    nn = d // bn
    extras = [_tile(x_res, bm, bn), _mod_op(lay, mod, gate_idx, bm, bn)]
    outs = [_tile(jax.ShapeDtypeStruct((t, d), jnp.float32), bm, bn)]
    epi = _glu_residual_epilogue if glu else _residual_epilogue
    return _matmul_wres(name, lhs, w, layer, [0, nn] if glu else [0], extras, outs, epi,
                        bm=bm, bn=bn, nn=nn, kblk=kblk, kidx=kidx,
                        epilogue_bytes=2 * _nbytes((bm, bn), jnp.float32))[0]


def _mlp(lay, h, w1, w2, layer, x_res, mod):
    t, d = x_res.shape
    dff = w1.shape[2]
    bm = lay.bm
    bn = _pick(dff, 1024)
    a = _matmul_wres("mlp_up", h, w1, layer, [0], [],
                     [_tile(jax.ShapeDtypeStruct((t, dff), jnp.bfloat16), bm, bn)],
                     _relu2_epilogue, bm=bm, bn=bn, nn=dff // bn,
                     epilogue_bytes=_nbytes((bm, bn), jnp.float32))[0]
    kblk = _pick(dff, d)
    for c in range(dff // kblk):
        x_res = _gated_residual_wres("mlp_down", lay, a, w2, layer, x_res, mod, 5,
                                     kblk=kblk, kidx=c)
    return x_res


def _cmul(ar, ai, br, bi):
    return ar * br - ai * bi, ar * bi + ai * br


def _expand_groups(compact, outer, inner, gpb, row_group):
    k = outer * inner
    col = jnp.arange(outer * gpb * inner)
    src = (col // (gpb * inner)) * inner + col % inner
    spread = (jnp.arange(k)[:, None] == src[None, :]).astype(compact.dtype)
    keep = row_group[:, None] == ((col // inner) % gpb)[None, :]
    full = jnp.einsum('brk,kn->brn', compact, spread, preferred_element_type=jnp.float32)
    return jnp.where(keep[None], full, 0.0).astype(compact.dtype)


def _s5_prepare(a_re, a_im, log_dt, b_re, b_im, c_re, c_im, d_skip):
    f32 = jnp.float32
    L = S5_CHUNK
    a_re = jnp.minimum(a_re.astype(f32), -1e-4)
    a_im = a_im.astype(f32)
    dt = jnp.exp(log_dt.astype(f32))[..., None]
    mag = jnp.exp(a_re * dt)
    ab_re = mag * jnp.cos(a_im * dt)
    ab_im = mag * jnp.sin(a_im * dt)
    den = a_re * a_re + a_im * a_im
    n_re = ab_re - 1.0
    q_re = (n_re * a_re + ab_im * a_im) / den
    q_im = (ab_im * a_re - n_re * a_im) / den
    b_re, b_im = b_re.astype(f32), b_im.astype(f32)
    bb_re = q_re[..., None] * b_re - q_im[..., None] * b_im
    bb_im = q_re[..., None] * b_im + q_im[..., None] * b_re
    c_re, c_im = c_re.astype(f32), c_im.astype(f32)
    bb_re, bb_im, c_re, c_im = [v.astype(jnp.bfloat16).astype(f32) for v in (bb_re, bb_im, c_re, c_im)]
    ngroups, nstate = ab_re.shape[1], ab_re.shape[2]
    cg = b_re.shape[-1]
    gpb = LANES // cg
    ncb = ngroups // gpb
    bf16 = jnp.bfloat16

    pr, pi = [jnp.ones_like(ab_re)], [jnp.zeros_like(ab_im)]
    for _ in range(L):
        nr, ni = _cmul(pr[-1], pi[-1], ab_re, ab_im)
        pr.append(nr)
        pi.append(ni)
    pw_re, pw_im = jnp.stack(pr), jnp.stack(pi)

    cp_re, cp_im = _cmul(c_re[None], c_im[None], pw_re[:L, :, :, None, :], pw_im[:L, :, :, None, :])
    kern = (jnp.einsum('kdgcp,dgpe->kdgce', cp_re, bb_re, precision=_HP)
            - jnp.einsum('kdgcp,dgpe->kdgce', cp_im, bb_im, precision=_HP))
    kf, kb = kern[:, 0], kern[:, 1]
    diag = jnp.eye(cg, dtype=f32)[None] * d_skip.astype(f32).reshape(ngroups, cg)[:, :, None]
    k_all = jnp.concatenate([kb[:0:-1], (kf[0] + kb[0] + diag)[None], kf[1:]], axis=0)
    s_idx = jnp.arange(L)[:, None]
    t_idx = jnp.arange(L)[None, :]
    tm = k_all[t_idx - s_idx + (L - 1)]
    tm = tm.transpose(2, 0, 4, 1, 3).astype(bf16)
    token_rows = (jnp.arange(L * LANES) // cg) % gpb
    state_rows = (jnp.arange(4 * gpb * nstate) // nstate) % gpb
    tc = tm.reshape(ncb, gpb, L, cg, L * cg).transpose(0, 2, 1, 3, 4).reshape(ncb, L * LANES, L * cg)
    t_cat = _expand_groups(tc, L, cg, gpb, token_rows)

    def win_dir(d, pows_re, pows_im):
        wr, wi = _cmul(pows_re[:, :, None, :], pows_im[:, :, None, :],
                       jnp.swapaxes(bb_re[d], 1, 2)[None], jnp.swapaxes(bb_im[d], 1, 2)[None])
        return wr, wi

    def pows(ks, d):
        return jnp.stack([pr[k][d] for k in ks]), jnp.stack([pi[k][d] for k in ks])

    wf_re, wf_im = win_dir(0, *pows(range(L - 1, -1, -1), 0))
    wb_re, wb_im = win_dir(1, *pows(range(L), 1))
    w4 = jnp.stack([wf_re, wf_im, wb_re, wb_im], axis=3).astype(bf16)
    wc = w4.reshape(L, ncb, gpb, cg, 4 * nstate).transpose(1, 0, 2, 3, 4)
    win_cat = _expand_groups(wc.reshape(ncb, L * LANES, 4 * nstate), 4, nstate, gpb, token_rows)

    def wout_dir(d, pows_re, pows_im):
        vr, vi = _cmul(c_re[d][None], c_im[d][None], pows_re[:, :, None, :], pows_im[:, :, None, :])
        return vr, vi

    vf_re, vf_im = wout_dir(0, *pows(range(1, L + 1), 0))
    vb_re, vb_im = wout_dir(1, *pows(range(L, 0, -1), 1))
    v4 = jnp.stack([vf_re, -vf_im, vb_re, -vb_im], axis=0).astype(bf16)
    vc = v4.reshape(4, L, ncb, gpb, cg, nstate).transpose(2, 0, 3, 5, 1, 4)
    wout_cat = _expand_groups(vc.reshape(ncb, 4 * gpb * nstate, L * cg), L, cg, gpb, state_rows)

    qr, qi = [jnp.ones_like(ab_re)], [jnp.zeros_like(ab_im)]
    for _ in range(SUBLANES):
        nr, ni = _cmul(qr[-1], qi[-1], pw_re[L], pw_im[L])
        qr.append(nr)
        qi.append(ni)
    asc, desc = range(SUBLANES), range(SUBLANES - 1, -1, -1)
    rows = ([qr[k][0] for k in asc] + [qi[k][0] for k in asc]
            + [qr[k][1] for k in desc] + [qi[k][1] for k in desc])
    for d in (0, 1):
        for k in (1, 2, 4, 8):
            rows += [qr[k][d], qi[k][d]]
    a8 = jnp.stack(rows, axis=0)
    a8 = a8.reshape(len(rows), ncb, gpb * nstate).transpose(1, 0, 2)
    return t_cat, win_cat, wout_cat, a8


def _s5_kernel(hl_ref, hc_ref, wz_ref, wt_ref, wo_ref, a_ref, yl_ref, yc_ref, u_ref, z_ref,
               *, jl, jc):
    L, lanes = S5_CHUNK, hl_ref.shape[1]
    for t in range(L):
        u_ref[0:jl, t * lanes:(t + 1) * lanes] = hl_ref[pl.ds(t, jl, stride=L), :].astype(u_ref.dtype)
        u_ref[jl:jl + jc, t * lanes:(t + 1) * lanes] = hc_ref[pl.ds(t, jc, stride=L), :].astype(u_ref.dtype)
    z_ref[...] = jnp.dot(u_ref[...], wz_ref[...], preferred_element_type=jnp.float32)

    sw = z_ref.shape[1] // 4
    rpb = SUBLANES
    rowid = lax.broadcasted_iota(jnp.int32, (rpb, sw), 0)

    def shift(x, k, down):
        if down:
            return jnp.where(rowid >= k, pltpu.roll(x, k, 0), 0.0)
        return jnp.where(rowid < rpb - k, pltpu.roll(x, rpb - k, 0), 0.0)

    def block_scan(zr, zi, sr, si, tab, down):
        apr, api = a_ref[tab:tab + rpb, :], a_ref[tab + rpb:tab + 2 * rpb, :]
        pw = SUBLANES * 4 + (0 if down else SUBLANES)
        xr, xi = zr, zi
        for n, k in enumerate((1, 2, 4)):
            ar, ai = a_ref[pw + 2 * n:pw + 2 * n + 1, :], a_ref[pw + 2 * n + 1:pw + 2 * n + 2, :]
            mr, mi = _cmul(ar, ai, shift(xr, k, down), shift(xi, k, down))
            xr, xi = xr + mr, xi + mi
        er, ei = _cmul(apr, api, sr, si)
        er, ei = er + shift(xr, 1, down), ei + shift(xi, 1, down)
        a8r, a8i = a_ref[pw + 6:pw + 7, :], a_ref[pw + 7:pw + 8, :]
        last = rpb - 1 if down else 0
        cr, ci = _cmul(a8r, a8i, sr, si)
        return er, ei, cr + xr[last:last + 1], ci + xi[last:last + 1]

    def sweep(n, base, carry):
        nblk = n // rpb

        def body(i, carry):
            sfr, sfi, sbr, sbi = carry
            rf = pl.multiple_of(base + i * rpb, rpb)
            rb = pl.multiple_of(base + (nblk - 1 - i) * rpb, rpb)
            efr, efi, sfr, sfi = block_scan(z_ref[pl.ds(rf, rpb), 0 * sw:1 * sw],
                                            z_ref[pl.ds(rf, rpb), 1 * sw:2 * sw], sfr, sfi, 0, True)
            ebr, ebi, sbr, sbi = block_scan(z_ref[pl.ds(rb, rpb), 2 * sw:3 * sw],
                                            z_ref[pl.ds(rb, rpb), 3 * sw:4 * sw], sbr, sbi,
                                            2 * rpb, False)
            z_ref[pl.ds(rf, rpb), 0 * sw:1 * sw] = efr
            z_ref[pl.ds(rf, rpb), 1 * sw:2 * sw] = efi
            z_ref[pl.ds(rb, rpb), 2 * sw:3 * sw] = ebr
            z_ref[pl.ds(rb, rpb), 3 * sw:4 * sw] = ebi
            return sfr, sfi, sbr, sbi

        return lax.fori_loop(0, nblk, body, carry)

    zero = jnp.zeros((1, sw), jnp.float32)
    carry = sweep(jc, jl, (zero, zero, zero, zero))
    sweep(jl, 0, carry)

    y = jnp.dot(u_ref[...], wt_ref[...], preferred_element_type=jnp.float32)
    y = y + jnp.dot(z_ref[...].astype(jnp.bfloat16), wo_ref[...], preferred_element_type=jnp.float32)
    y = jax.nn.gelu(y)
    for t in range(L):
        yl_ref[pl.ds(t, jl, stride=L), :] = y[0:jl, t * lanes:(t + 1) * lanes].astype(yl_ref.dtype)
        yc_ref[pl.ds(t, jc, stride=L), :] = y[jl:jl + jc, t * lanes:(t + 1) * lanes].astype(yc_ref.dtype)


def _s5_mixer(lay, h, params, out_dtype):
    t_cat, win_cat, wout_cat, a8 = params
    b, d = lay.batch, lay.d
    L = S5_CHUNK
    jl, jc = lay.seq // L, lay.ctx // L
    assert jl % SUBLANES == 0 and jc % SUBLANES == 0 and (b * jl) % jc == 0
    cw = L * LANES
    sw4 = win_cat.shape[2]
    rows = jl + jc
    ctx0 = lay.tl // lay.ctx
    once = pl.Buffered(1)
    vmem = 2 * _nbytes((rows, L, LANES), h.dtype) + 2 * _nbytes((rows, L, LANES), out_dtype) \
        + 2 * _nbytes((cw, cw), jnp.bfloat16) + _nbytes((cw, sw4), jnp.bfloat16) \
        + _nbytes((rows, cw), jnp.bfloat16) + 4 * _nbytes((rows, max(cw, sw4)), jnp.float32)
    yl, yc = pl.pallas_call(
        functools.partial(_s5_kernel, jl=jl, jc=jc),
        grid=(d // LANES, b),
        in_specs=[
            pl.BlockSpec((lay.seq, LANES), lambda cb, bi: (bi, cb)),
            pl.BlockSpec((lay.ctx, LANES), lambda cb, bi: (ctx0 + bi, cb)),
            pl.BlockSpec((None, cw, sw4), lambda cb, bi: (cb, 0, 0), pipeline_mode=once),
            pl.BlockSpec((None, cw, cw), lambda cb, bi: (cb, 0, 0), pipeline_mode=once),
            pl.BlockSpec((None, sw4, cw), lambda cb, bi: (cb, 0, 0), pipeline_mode=once),
            pl.BlockSpec((None, a8.shape[1], a8.shape[2]), lambda cb, bi: (cb, 0, 0)),
        ],
        out_specs=[pl.BlockSpec((lay.seq, LANES), lambda cb, bi: (bi, cb)),
                   pl.BlockSpec((lay.ctx, LANES), lambda cb, bi: (bi, cb))],
        out_shape=[jax.ShapeDtypeStruct((lay.tl, d), out_dtype),
                   jax.ShapeDtypeStruct((lay.tc, d), out_dtype)],
        scratch_shapes=[pltpu.VMEM((rows, cw), jnp.bfloat16), pltpu.VMEM((rows, sw4), jnp.float32)],
        compiler_params=_cparams(("arbitrary", "arbitrary"), vmem),
        name="s5_chunked",
    )(h, h, win_cat, t_cat, wout_cat, a8)
    return jnp.concatenate([yl, yc], axis=0).astype(jnp.bfloat16)


def _rope_tables(lay, bm):
    n = lay.seq
    rows = n // GRID_W
    row = jnp.broadcast_to(jnp.arange(rows)[:, None], (rows, GRID_W)).reshape(-1)
    col = jnp.broadcast_to(jnp.arange(GRID_W)[None, :], (rows, GRID_W)).reshape(-1)
    n_freq = HEAD_DIM // 4
    inv_freq = ROPE_THETA ** (-jnp.arange(n_freq, dtype=jnp.float32) / n_freq)
    ang_r = row.astype(jnp.float32)[:, None] * inv_freq
    ang_c = col.astype(jnp.float32)[:, None] * inv_freq
    cr, sr, cc, sc = jnp.cos(ang_r), jnp.sin(ang_r), jnp.cos(ang_c), jnp.sin(ang_c)
    cos = jnp.concatenate([cr, cr, cc, cc], axis=-1)
    sin = jnp.concatenate([-sr, sr, -sc, sc], axis=-1)
    cos = jnp.concatenate([cos, jnp.ones((bm, HEAD_DIM), jnp.float32)], axis=0)
    sin = jnp.concatenate([sin, jnp.zeros((bm, HEAD_DIM), jnp.float32)], axis=0)
    return cos, sin


def _norm_rope(x, gain, cos, sin, scale):
    ms = jnp.mean(x * x, axis=-1, keepdims=True)
    xn = x * lax.rsqrt(ms + EPS) * gain
    q4 = HEAD_DIM // 4
    lane = lax.broadcasted_iota(jnp.int32, xn.shape, 1)
    partner = jnp.where(lane % (2 * q4) < q4,
                        pltpu.roll(xn, HEAD_DIM - q4, 1),
                        pltpu.roll(xn, q4, 1))
    out = xn * cos + partner * sin
    return out * scale if scale != 1.0 else out


def _q_epilogue(accs, extra_refs, out_refs, i, j, *, scale):
    gain_ref, cos_ref, sin_ref = extra_refs
    acc = accs[0]
    cos, sin, gain = cos_ref[...], sin_ref[...], gain_ref[...]
    for hh in range(acc.shape[1] // HEAD_DIM):
        sl = slice(hh * HEAD_DIM, (hh + 1) * HEAD_DIM)
        out_refs[0][:, sl] = _norm_rope(acc[:, sl], gain, cos, sin, scale).astype(out_refs[0].dtype)


def _kv_epilogue(accs, extra_refs, out_refs, i, j):
    gain_ref, cos_ref, sin_ref = extra_refs
    k_acc, v_acc = accs
    cos, sin, gain = cos_ref[...], sin_ref[...], gain_ref[...]
    for hh in range(k_acc.shape[1] // HEAD_DIM):
        sl = slice(hh * HEAD_DIM, (hh + 1) * HEAD_DIM)
        out_refs[0][:, sl] = _norm_rope(k_acc[:, sl], gain, cos, sin, 1.0).astype(out_refs[0].dtype)
    out_refs[1][...] = v_acc.astype(out_refs[1].dtype)


def _attn_kernel(q_ref, kx_ref, kc_ref, vx_ref, vc_ref, o_ref, *, with_latent):
    dn = (((1,), (1,)), ((), ()))
    for r in range(KV_REP):
        sl = slice(r * HEAD_DIM, (r + 1) * HEAD_DIM)
        q = q_ref[:, sl]
        sc = lax.dot_general(q, kc_ref[...], dn, preferred_element_type=jnp.float32)
        m = jnp.max(sc, axis=-1, keepdims=True)
        if with_latent:
            sx = lax.dot_general(q, kx_ref[...], dn, preferred_element_type=jnp.float32)
            m = jnp.maximum(m, jnp.max(sx, axis=-1, keepdims=True))
            px = jnp.exp(sx - m)
        pc = jnp.exp(sc - m)
        l = jnp.sum(pc, axis=-1, keepdims=True)
        o = jnp.dot(pc.astype(vc_ref.dtype), vc_ref[...], preferred_element_type=jnp.float32)
        if with_latent:
            l = l + jnp.sum(px, axis=-1, keepdims=True)
            o = o + jnp.dot(px.astype(vx_ref.dtype), vx_ref[...], preferred_element_type=jnp.float32)
        o_ref[:, sl] = (o / l).astype(o_ref.dtype)


def _attention(lay, q, k, v):
    t, dq = q.shape
    nkv = k.shape[1] // HEAD_DIM
    b, seq, ctx = lay.batch, lay.seq, lay.ctx
    qw = KV_REP * HEAD_DIM
    tq = _pick(seq, 256, SUBLANES)
    nq = seq // tq
    cblk0 = lay.tl // ctx
    kx_spec = pl.BlockSpec((seq, HEAD_DIM), lambda bi, g, qi: (bi, g))
    kc_spec = pl.BlockSpec((ctx, HEAD_DIM), lambda bi, g, qi: (cblk0 + bi, g))
    vmem = 4 * _nbytes((tq, qw), jnp.bfloat16) + 8 * _nbytes((seq + ctx, HEAD_DIM), jnp.bfloat16) \
        + 4 * _nbytes((tq, seq + ctx), jnp.float32)
    o_lat = pl.pallas_call(
        functools.partial(_attn_kernel, with_latent=True),
        grid=(b, nkv, nq),
        in_specs=[pl.BlockSpec((tq, qw), lambda bi, g, qi: (bi * nq + qi, g)),
                  kx_spec, kc_spec, kx_spec, kc_spec],
        out_specs=pl.BlockSpec((tq, qw), lambda bi, g, qi: (bi * nq + qi, g)),
        out_shape=jax.ShapeDtypeStruct((t, dq), jnp.bfloat16),
        compiler_params=_cparams(("parallel", "parallel", "arbitrary"), vmem),
        name="attn_latent",
    )(q, k, k, v, v)
    kc2 = pl.BlockSpec((ctx, HEAD_DIM), lambda bi, g: (cblk0 + bi, g))

    def ctx_kernel(q_ref, kc_ref, vc_ref, prev_ref, o_ref):
        del prev_ref
        _attn_kernel(q_ref, None, kc_ref, None, vc_ref, o_ref, with_latent=False)

    return pl.pallas_call(
        ctx_kernel,
        grid=(b, nkv),
        in_specs=[pl.BlockSpec((ctx, qw), lambda bi, g: (cblk0 + bi, g)), kc2, kc2,
                  pl.BlockSpec(memory_space=pl.ANY)],
        out_specs=pl.BlockSpec((ctx, qw), lambda bi, g: (cblk0 + bi, g)),
        out_shape=jax.ShapeDtypeStruct((t, dq), jnp.bfloat16),
        input_output_aliases={3: 0},
        compiler_params=_cparams(("parallel", "parallel"), 16 * 1024 * 1024),
        name="attn_context",
    )(q, k, v, o_lat)


def _attention_mixer(lay, h, w_qkv, layer, dq, dkv, q_gain, k_gain):
    t, d = h.shape
    bm = lay.bm
    cos, sin = _rope_tables(lay, bm)
    nseq = lay.seq // bm

    def pos_block(i):
        return jnp.where(i * bm < lay.tl, i % nseq, nseq)

    tab = lambda arr: (arr, (bm, HEAD_DIM), lambda i, j: (pos_block(i), 0))
    gain = lambda g: (g.reshape(1, HEAD_DIM), (1, HEAD_DIM), lambda i, j: (0, 0))
    bn = _pick(dq, 512)
    q = _matmul_wres("attn_q", h, w_qkv, layer, [0], [gain(q_gain), tab(cos), tab(sin)],
                     [_tile(jax.ShapeDtypeStruct((t, dq), jnp.bfloat16), bm, bn)],
                     functools.partial(_q_epilogue, scale=HEAD_DIM ** -0.5),
                     bm=bm, bn=bn, nn=dq // bn, epilogue_bytes=4 * _nbytes((bm, bn), jnp.float32))[0]
    bn = _pick(dkv, 256)
    o_sds = jax.ShapeDtypeStruct((t, dkv), jnp.bfloat16)
    k, v = _matmul_wres("attn_kv", h, w_qkv, layer, [dq // bn, (dq + dkv) // bn],
                        [gain(k_gain), tab(cos), tab(sin)],
                        [_tile(o_sds, bm, bn), _tile(o_sds, bm, bn)],
                        _kv_epilogue, bm=bm, bn=bn, nn=dkv // bn,
                        epilogue_bytes=4 * _nbytes((bm, bn), jnp.float32))
    return _attention(lay, q, k, v)


def _conv_in_epilogue(accs, extra_refs, out_refs, i, j):
    b_gate, c_gate, v = accs
    out_refs[0][...] = b_gate.astype(out_refs[0].dtype)
    out_refs[1][...] = (c_gate * v).astype(out_refs[1].dtype)


def _conv_kernel(b_ref, u_ref, up_ref, un_ref, w_ref, o_ref, *, rb, lay):
    i = pl.program_id(0)
    row0 = i * rb
    is_lat = row0 < lay.tl
    seq_len = jnp.where(is_lat, lay.seq, lay.ctx)
    local = jnp.where(is_lat, row0, row0 - lay.tl) % seq_len
    first = local == 0
    last = local + rb == seq_len
    u = u_ref[...].astype(jnp.float32)
    prev_row = jnp.where(first, 0.0, up_ref[SUBLANES - 1:SUBLANES, :].astype(jnp.float32))
    next_row = jnp.where(last, 0.0, un_ref[0:1, :].astype(jnp.float32))
    ridx = lax.broadcasted_iota(jnp.int32, u.shape, 0)
    u_prev = jnp.where(ridx == 0, prev_row, pltpu.roll(u, 1, 0))
    u_next = jnp.where(ridx == rb - 1, next_row, pltpu.roll(u, rb - 1, 0))
    w = w_ref[...]
    conv = w[0:1, :] * u_prev + w[1:2, :] * u + w[2:3, :] * u_next
    o_ref[...] = (b_ref[...].astype(jnp.float32) * conv).astype(o_ref.dtype)


def _short_conv(lay, b_gate, u, conv_w):
    t, d = u.shape
    rb = _pick(lay.ctx, 256, SUBLANES)
    assert lay.ctx % rb == 0 and lay.seq % rb == 0
    bc = _pick(d, 1024)
    sub = rb // SUBLANES
    nsub = t // SUBLANES
    w8 = jnp.concatenate([conv_w.astype(jnp.float32),
                          jnp.zeros((SUBLANES - conv_w.shape[0], d), jnp.float32)], axis=0)
    return pl.pallas_call(
        functools.partial(_conv_kernel, rb=rb, lay=lay),
        grid=(t // rb, d // bc),
        in_specs=[
            pl.BlockSpec((rb, bc), lambda i, j: (i, j)),
            pl.BlockSpec((rb, bc), lambda i, j: (i, j)),
            pl.BlockSpec((SUBLANES, bc), lambda i, j: (jnp.maximum(i * sub - 1, 0), j)),
            pl.BlockSpec((SUBLANES, bc), lambda i, j: (jnp.minimum((i + 1) * sub, nsub - 1), j)),
            pl.BlockSpec((SUBLANES, bc), lambda i, j: (0, j)),
        ],
        out_specs=pl.BlockSpec((rb, bc), lambda i, j: (i, j)),
        out_shape=jax.ShapeDtypeStruct((t, d), jnp.bfloat16),
        compiler_params=_cparams(("parallel", "parallel"),
                                 6 * _nbytes((rb, bc), jnp.bfloat16) + 8 * _nbytes((rb, bc), jnp.float32)),
        name="short_conv",
    )(b_gate, u, u, u, w8)


def _conv_mixer(lay, h, w_in, layer, conv_w):
    t, d = h.shape
    bm = lay.bm
    bn = _pick(d, 256)
    nn = d // bn
    o_sds = jax.ShapeDtypeStruct((t, d), jnp.bfloat16)
    b_gate, u = _matmul_wres("conv_in", h, w_in, layer, [0, nn, 2 * nn], [],
                             [_tile(o_sds, bm, bn), _tile(o_sds, bm, bn)],
                             _conv_in_epilogue, bm=bm, bn=bn, nn=nn,
                             epilogue_bytes=_nbytes((bm, bn), jnp.float32))
    return _short_conv(lay, b_gate, u, conv_w)


def kernel(x, c, ctx, c_ctx, ada_down, ada_up, ada_bias, norm1_g, norm2_g, s5_a_re, s5_a_im, s5_log_dt, s5_b_re, s5_b_im, s5_c_re, s5_c_im, s5_d, s5_w_glu, attn_w_qkv, attn_q_gain, attn_k_gain, attn_w_o, conv_w_in, conv_w, conv_w_out, mlp_w1, mlp_w2):
    batch, seq, d = x.shape
    n_ctx = ctx.shape[1]
    depth = ada_down.shape[0]
    lay = _Layout(batch, seq, n_ctx, d)
    bf16 = jnp.bfloat16

    xs = jnp.concatenate([x.reshape(lay.tl, d), ctx.reshape(lay.tc, d)], axis=0)

    cond = jnp.concatenate([c, c_ctx[None, :],
                            jnp.zeros((MOD_ROWS - batch - 1, d), c.dtype)], axis=0)
    mods = _ada_all(cond, ada_down, ada_up, ada_bias)
    mods = mods.reshape(depth, MOD_ROWS, N_MOD, 1, d).transpose(0, 2, 1, 3, 4)

    n_mixers = 3
    for i in range(depth):
        kind, j = i % n_mixers, i // n_mixers
        mod = mods[i]
        h = _norm_mod(lay, xs, norm1_g[i], mod, 0, 1,
                      out_dtype=jnp.float32 if kind == 0 else bf16)
        if kind == 0:
            params = _s5_prepare(s5_a_re[j], s5_a_im[j], s5_log_dt[j], s5_b_re[j], s5_b_im[j],
                                 s5_c_re[j], s5_c_im[j], s5_d[j])
            y = _s5_mixer(lay, h, params, jnp.float32)
            xs = _gated_residual_wres("s5_glu", lay, y, s5_w_glu, j, xs, mod, 2, glu=True)
        elif kind == 1:
            dq = attn_w_o.shape[1]
            dkv = (attn_w_qkv.shape[2] - dq) // 2
            o = _attention_mixer(lay, h, attn_w_qkv, j, dq, dkv, attn_q_gain[j], attn_k_gain[j])
            xs = _gated_residual_wres("attn_out", lay, o, attn_w_o, j, xs, mod, 2)
        else:
            bu = _conv_mixer(lay, h, conv_w_in, j, conv_w[j])
            xs = _gated_residual_wres("conv_out", lay, bu, conv_w_out, j, xs, mod, 2)
        h = _norm_mod(lay, xs, norm2_g[i], mod, 3, 4)
        xs = _mlp(lay, h, mlp_w1, mlp_w2, i, xs, mod)
    return xs[:lay.tl].reshape(batch, seq, d)
```

```python
import functools

import jax
import jax.numpy as jnp
from jax import lax
from jax.experimental import pallas as pl
from jax.experimental.pallas import tpu as pltpu

EPS = 1e-6
N_MOD = 6
S5_GROUP = 16
S5_CHUNK = 16
HEAD_DIM = 128
KV_REP = 4
GRID_W = 64
ROPE_THETA = 10000.0
LANES = 128
SUBLANES = 8
V7X_VMEM_BYTES = 64 * 1024 * 1024
VMEM_CEILING = V7X_VMEM_BYTES - 4 * 1024 * 1024
MOD_ROWS = 8

_HP = lax.Precision.HIGHEST


INTERNAL_SCRATCH_BYTES = 2 * 1024 * 1024


def _cparams(sem, vmem_bytes):
    vmem_bytes += INTERNAL_SCRATCH_BYTES
    return pltpu.CompilerParams(
        dimension_semantics=sem,
        vmem_limit_bytes=int(min(max(vmem_bytes, 16 * 1024 * 1024), VMEM_CEILING)))


def _nbytes(shape, dtype):
    n = 1
    for s in shape:
        if s is not None:
            n *= s
    return n * jnp.dtype(dtype).itemsize


def _pick(n, target, mult=LANES):
    best = None
    for d in range(mult, min(n, target) + 1, mult):
        if n % d == 0:
            best = d
    return best if best is not None else n


def _ada_kernel(cond_ref, down_ref, up_ref, b_ref, o_ref, t_ref):
    @pl.when(pl.program_id(1) == 0)
    def _():
        cnd = cond_ref[...]
        s = cnd * jax.nn.sigmoid(cnd)
        t_ref[...] = jnp.dot(s.astype(jnp.bfloat16), down_ref[...].astype(jnp.bfloat16),
                             preferred_element_type=jnp.float32)

    o_ref[...] = jnp.dot(t_ref[...].astype(jnp.bfloat16), up_ref[...].astype(jnp.bfloat16),
                         preferred_element_type=jnp.float32) + b_ref[...]


def _ada_all(cond, ada_down, ada_up, ada_bias):
    depth, d, r = ada_down.shape
    n = ada_up.shape[2]
    bn = _pick(n, 2048)
    vmem = 2 * (_nbytes((MOD_ROWS, d), jnp.float32) + _nbytes((d, r), jnp.float32)
                + _nbytes((r, bn), jnp.float32) + 2 * _nbytes((MOD_ROWS, bn), jnp.float32))
    return pl.pallas_call(
        _ada_kernel,
        grid=(depth, n // bn),
        in_specs=[
            pl.BlockSpec((MOD_ROWS, d), lambda l, j: (0, 0)),
            pl.BlockSpec((None, d, r), lambda l, j: (l, 0, 0)),
            pl.BlockSpec((None, r, bn), lambda l, j: (l, 0, j)),
            pl.BlockSpec((None, 1, bn), lambda l, j: (l, 0, j)),
        ],
        out_specs=pl.BlockSpec((None, MOD_ROWS, bn), lambda l, j: (l, 0, j)),
        out_shape=jax.ShapeDtypeStruct((depth, MOD_ROWS, n), jnp.float32),
        scratch_shapes=[pltpu.VMEM((MOD_ROWS, r), jnp.float32)],
        compiler_params=_cparams(("arbitrary", "arbitrary"), 2 * vmem),
        name="ada_mod",
    )(cond, ada_down, ada_up, ada_bias.reshape(depth, 1, n))


def _norm_kernel(x_ref, g_ref, sh_ref, sc_ref, o_ref):
    x = x_ref[...]
    ms = jnp.mean(x * x, axis=-1, keepdims=True)
    y = x * lax.rsqrt(ms + EPS) * g_ref[...]
    o_ref[...] = (y * (1.0 + sc_ref[...]) + sh_ref[...]).astype(o_ref.dtype)


class _Layout:
    def __init__(self, batch, seq, ctx, d):
        self.batch, self.seq, self.ctx, self.d = batch, seq, ctx, d
        self.tl, self.tc = batch * seq, batch * ctx
        self.t = self.tl + self.tc
        self.bm = _pick(self.tc, 1024, SUBLANES)
        assert seq % self.bm == 0 and self.tc % self.bm == 0
        assert batch + 1 <= MOD_ROWS

    def mod_row(self, i, bm):
        return jnp.where(i * bm < self.tl, (i * bm) // self.seq, self.batch)


def _norm_mod(lay, x, g, mod, shift_idx, scale_idx, out_dtype=jnp.bfloat16):
    t, d = x.shape
    bm = _pick(lay.bm, 256, SUBLANES)
    vmem = 2 * (_nbytes((bm, d), jnp.float32) + _nbytes((bm, d), out_dtype)) \
        + 4 * _nbytes((bm, d), jnp.float32)

    def mod_spec(which):
        return pl.BlockSpec((None, None, 1, d), lambda i: (which, lay.mod_row(i, bm), 0, 0))

    return pl.pallas_call(
        _norm_kernel,
        grid=(t // bm,),
        in_specs=[
            pl.BlockSpec((bm, d), lambda i: (i, 0)),
            pl.BlockSpec((1, d), lambda i: (0, 0)),
            mod_spec(shift_idx),
            mod_spec(scale_idx),
        ],
        out_specs=pl.BlockSpec((bm, d), lambda i: (i, 0)),
        out_shape=jax.ShapeDtypeStruct((t, d), out_dtype),
        compiler_params=_cparams(("parallel",), vmem),
        name="norm_mod",
    )(x, g.reshape(1, d), mod, mod)


def _tile(arr, bm, bn):
    return (arr, (bm, bn), lambda i, j: (i, j))


def _mod_op(lay, mod, which, bm, bn):
    return (mod, (None, None, 1, bn), lambda i, j: (which, lay.mod_row(i, bm), 0, j))


def _mm_wres_kernel(*refs, n_rhs, n_extra, n_out, epilogue):
    lhs_ref = refs[0]
    w_refs = refs[1:1 + n_rhs]
    extra_refs = refs[1 + n_rhs:1 + n_rhs + n_extra]
    out_refs = refs[1 + n_rhs + n_extra:1 + n_rhs + n_extra + n_out]
    wbf_refs = refs[1 + n_rhs + n_extra + n_out:]
    j, i = pl.program_id(0), pl.program_id(1)

    @pl.when(i == 0)
    def _():
        for wb, w in zip(wbf_refs, w_refs):
            wb[...] = w[...].astype(wb.dtype)

    accs = [jnp.dot(lhs_ref[...], wb[...], preferred_element_type=jnp.float32) for wb in wbf_refs]
    epilogue(accs, extra_refs, out_refs, i, j)


def _matmul_wres(name, lhs, w, layer, offs, extras, outs, epilogue, *, bm, bn, nn,
                 kblk=None, kidx=0, w_buffers=2, epilogue_bytes=0):
    m = outs[0][0].shape[0]
    kdim = lhs.shape[1] if kblk is None else kblk
    assert m % bm == 0 and m <= lhs.shape[0]
    assert w.shape[1] == lhs.shape[1] and lhs.shape[1] % kdim == 0
    w_mode = {"pipeline_mode": pl.Buffered(1)} if w_buffers == 1 else {}

    def bind(shape, fn):
        return pl.BlockSpec(shape, lambda j, i: fn(i, j))

    in_specs = [pl.BlockSpec((bm, kdim), lambda j, i: (i, kidx))]
    for off in offs:
        in_specs.append(pl.BlockSpec((None, kdim, bn), lambda j, i, off=off: (layer, kidx, off + j),
                                     **w_mode))
    in_specs += [bind(s, f) for _, s, f in extras]
    vmem = 2 * _nbytes((bm, kdim), lhs.dtype)
    vmem += len(offs) * (w_buffers * _nbytes((kdim, bn), w.dtype) + _nbytes((kdim, bn), jnp.bfloat16))
    vmem += sum(2 * _nbytes(s, a.dtype) for a, s, _ in extras + outs)
    vmem += 2 * len(offs) * _nbytes((bm, bn), jnp.float32) + epilogue_bytes
    kern = functools.partial(_mm_wres_kernel, n_rhs=len(offs), n_extra=len(extras),
                             n_out=len(outs), epilogue=epilogue)
    return pl.pallas_call(
        kern,
        grid=(nn, m // bm),
        in_specs=in_specs,
        out_specs=[bind(s, f) for _, s, f in outs],
        out_shape=[a for a, _, _ in outs],
        scratch_shapes=[pltpu.VMEM((kdim, bn), jnp.bfloat16) for _ in offs],
        compiler_params=_cparams(("arbitrary", "arbitrary"), vmem),
        name=name,
    )(lhs, *[w for _ in offs], *[a for a, _, _ in extras])


def _residual_epilogue(accs, extra_refs, out_refs, i, j):
    res_ref, gate_ref = extra_refs
    out_refs[0][...] = res_ref[...] + gate_ref[...] * accs[0]


def _glu_residual_epilogue(accs, extra_refs, out_refs, i, j):
    res_ref, gate_ref = extra_refs
    a, g = accs
    out_refs[0][...] = res_ref[...] + gate_ref[...] * (a * jax.nn.sigmoid(g))


def _relu2_epilogue(accs, extra_refs, out_refs, i, j):
    r = jnp.maximum(accs[0], 0.0)
    out_refs[0][...] = (r * r).astype(out_refs[0].dtype)


def _gated_residual_wres(name, lay, lhs, w, layer, x_res, mod, gate_idx, *, glu=False,
                         kblk=None, kidx=0):
    t, d = x_res.shape
    bm = lay.bm
    bn = _pick(d, 256 if glu else 512)
    nn = d // bn
    extras = [_tile(x_res, bm, bn), _mod_op(lay, mod, gate_idx, bm, bn)]
    outs = [_tile(jax.ShapeDtypeStruct((t, d), jnp.float32), bm, bn)]
    epi = _glu_residual_epilogue if glu else _residual_epilogue
    return _matmul_wres(name, lhs, w, layer, [0, nn] if glu else [0], extras, outs, epi,
                        bm=bm, bn=bn, nn=nn, kblk=kblk, kidx=kidx,
                        epilogue_bytes=2 * _nbytes((bm, bn), jnp.float32))[0]


def _mlp(lay, h, w1, w2, layer, x_res, mod):
    t, d = x_res.shape
    dff = w1.shape[2]
    bm = lay.bm
    bn = _pick(dff, 1024)
    a = _matmul_wres("mlp_up", h, w1, layer, [0], [],
                     [_tile(jax.ShapeDtypeStruct((t, dff), jnp.bfloat16), bm, bn)],
                     _relu2_epilogue, bm=bm, bn=bn, nn=dff // bn, w_buffers=1,
                     epilogue_bytes=_nbytes((bm, bn), jnp.float32))[0]
    kblk = _pick(dff, d)
    for c in range(dff // kblk):
        x_res = _gated_residual_wres("mlp_down", lay, a, w2, layer, x_res, mod, 5,
                                     kblk=kblk, kidx=c)
    return x_res


def _cmul(ar, ai, br, bi):
    return ar * br - ai * bi, ar * bi + ai * br


def _expand_groups(compact, outer, inner, gpb, row_group):
    k = outer * inner
    col = jnp.arange(outer * gpb * inner)
    src = (col // (gpb * inner)) * inner + col % inner
    spread = (jnp.arange(k)[:, None] == src[None, :]).astype(compact.dtype)
    keep = row_group[:, None] == ((col // inner) % gpb)[None, :]
    full = jnp.einsum('brk,kn->brn', compact, spread, preferred_element_type=jnp.float32)
    return jnp.where(keep[None], full, 0.0).astype(compact.dtype)


def _s5_prepare(a_re, a_im, log_dt, b_re, b_im, c_re, c_im, d_skip):
    f32 = jnp.float32
    L = S5_CHUNK
    a_re = jnp.minimum(a_re.astype(f32), -1e-4)
    a_im = a_im.astype(f32)
    dt = jnp.exp(log_dt.astype(f32))[..., None]
    mag = jnp.exp(a_re * dt)
    ab_re = mag * jnp.cos(a_im * dt)
    ab_im = mag * jnp.sin(a_im * dt)
    den = a_re * a_re + a_im * a_im
    n_re = ab_re - 1.0
    q_re = (n_re * a_re + ab_im * a_im) / den
    q_im = (ab_im * a_re - n_re * a_im) / den
    b_re, b_im = b_re.astype(f32), b_im.astype(f32)
    bb_re = q_re[..., None] * b_re - q_im[..., None] * b_im
    bb_im = q_re[..., None] * b_im + q_im[..., None] * b_re
    c_re, c_im = c_re.astype(f32), c_im.astype(f32)
    bb_re, bb_im, c_re, c_im = [v.astype(jnp.bfloat16).astype(f32) for v in (bb_re, bb_im, c_re, c_im)]
    ngroups, nstate = ab_re.shape[1], ab_re.shape[2]
    cg = b_re.shape[-1]
    gpb = LANES // cg
    ncb = ngroups // gpb
    bf16 = jnp.bfloat16

    pr, pi = [jnp.ones_like(ab_re)], [jnp.zeros_like(ab_im)]
    for _ in range(L):
        nr, ni = _cmul(pr[-1], pi[-1], ab_re, ab_im)
        pr.append(nr)
        pi.append(ni)
    pw_re, pw_im = jnp.stack(pr), jnp.stack(pi)

    cp_re, cp_im = _cmul(c_re[None], c_im[None], pw_re[:L, :, :, None, :], pw_im[:L, :, :, None, :])
    kern = (jnp.einsum('kdgcp,dgpe->kdgce', cp_re, bb_re, precision=_HP)
            - jnp.einsum('kdgcp,dgpe->kdgce', cp_im, bb_im, precision=_HP))
    kf, kb = kern[:, 0], kern[:, 1]
    diag = jnp.eye(cg, dtype=f32)[None] * d_skip.astype(f32).reshape(ngroups, cg)[:, :, None]
    k_all = jnp.concatenate([kb[:0:-1], (kf[0] + kb[0] + diag)[None], kf[1:]], axis=0)
    s_idx = jnp.arange(L)[:, None]
    t_idx = jnp.arange(L)[None, :]
    tm = k_all[t_idx - s_idx + (L - 1)]
    tm = tm.transpose(2, 0, 4, 1, 3).astype(bf16)
    token_rows = (jnp.arange(L * LANES) // cg) % gpb
    state_rows = (jnp.arange(4 * gpb * nstate) // nstate) % gpb
    tc = tm.reshape(ncb, gpb, L, cg, L * cg).transpose(0, 2, 1, 3, 4).reshape(ncb, L * LANES, L * cg)
    t_cat = _expand_groups(tc, L, cg, gpb, token_rows)

    def win_dir(d, pows_re, pows_im):
        wr, wi = _cmul(pows_re[:, :, None, :], pows_im[:, :, None, :],
                       jnp.swapaxes(bb_re[d], 1, 2)[None], jnp.swapaxes(bb_im[d], 1, 2)[None])
        return wr, wi

    def pows(ks, d):
        return jnp.stack([pr[k][d] for k in ks]), jnp.stack([pi[k][d] for k in ks])

    wf_re, wf_im = win_dir(0, *pows(range(L - 1, -1, -1), 0))
    wb_re, wb_im = win_dir(1, *pows(range(L), 1))
    w4 = jnp.stack([wf_re, wf_im, wb_re, wb_im], axis=3).astype(bf16)
    wc = w4.reshape(L, ncb, gpb, cg, 4 * nstate).transpose(1, 0, 2, 3, 4)
    win_cat = _expand_groups(wc.reshape(ncb, L * LANES, 4 * nstate), 4, nstate, gpb, token_rows)

    def wout_dir(d, pows_re, pows_im):
        vr, vi = _cmul(c_re[d][None], c_im[d][None], pows_re[:, :, None, :], pows_im[:, :, None, :])
        return vr, vi

    vf_re, vf_im = wout_dir(0, *pows(range(1, L + 1), 0))
    vb_re, vb_im = wout_dir(1, *pows(range(L, 0, -1), 1))
    v4 = jnp.stack([vf_re, -vf_im, vb_re, -vb_im], axis=0).astype(bf16)
    vc = v4.reshape(4, L, ncb, gpb, cg, nstate).transpose(2, 0, 3, 5, 1, 4)
    wout_cat = _expand_groups(vc.reshape(ncb, 4 * gpb * nstate, L * cg), L, cg, gpb, state_rows)

    qr, qi = [jnp.ones_like(ab_re)], [jnp.zeros_like(ab_im)]
    for _ in range(SUBLANES):
        nr, ni = _cmul(qr[-1], qi[-1], pw_re[L], pw_im[L])
        qr.append(nr)
        qi.append(ni)
    asc, desc = range(SUBLANES), range(SUBLANES - 1, -1, -1)
    rows = ([qr[k][0] for k in asc] + [qi[k][0] for k in asc]
            + [qr[k][1] for k in desc] + [qi[k][1] for k in desc])
    for d in (0, 1):
        for k in (1, 2, 4, 8):
            rows += [qr[k][d], qi[k][d]]
    a8 = jnp.stack(rows, axis=0)
    a8 = a8.reshape(len(rows), ncb, gpb * nstate).transpose(1, 0, 2)
    return t_cat, win_cat, wout_cat, a8


def _s5_kernel(hl_ref, hc_ref, wz_ref, wt_ref, wo_ref, a_ref, yl_ref, yc_ref, u_ref, z_ref,
               *, jl, jc):
    L, lanes = S5_CHUNK, hl_ref.shape[1]
    for t in range(L):
        u_ref[0:jl, t * lanes:(t + 1) * lanes] = hl_ref[pl.ds(t, jl, stride=L), :].astype(u_ref.dtype)
        u_ref[jl:jl + jc, t * lanes:(t + 1) * lanes] = hc_ref[pl.ds(t, jc, stride=L), :].astype(u_ref.dtype)
    z_ref[...] = jnp.dot(u_ref[...], wz_ref[...], preferred_element_type=jnp.float32)

    sw = z_ref.shape[1] // 4
    rpb = SUBLANES
    rowid = lax.broadcasted_iota(jnp.int32, (rpb, sw), 0)

    def shift(x, k, down):
        if down:
            return jnp.where(rowid >= k, pltpu.roll(x, k, 0), 0.0)
        return jnp.where(rowid < rpb - k, pltpu.roll(x, rpb - k, 0), 0.0)

    def block_scan(zr, zi, sr, si, tab, down):
        apr, api = a_ref[tab:tab + rpb, :], a_ref[tab + rpb:tab + 2 * rpb, :]
        pw = SUBLANES * 4 + (0 if down else SUBLANES)
        xr, xi = zr, zi
        for n, k in enumerate((1, 2, 4)):
            ar, ai = a_ref[pw + 2 * n:pw + 2 * n + 1, :], a_ref[pw + 2 * n + 1:pw + 2 * n + 2, :]
            mr, mi = _cmul(ar, ai, shift(xr, k, down), shift(xi, k, down))
            xr, xi = xr + mr, xi + mi
        er, ei = _cmul(apr, api, sr, si)
        er, ei = er + shift(xr, 1, down), ei + shift(xi, 1, down)
        a8r, a8i = a_ref[pw + 6:pw + 7, :], a_ref[pw + 7:pw + 8, :]
        last = rpb - 1 if down else 0
        cr, ci = _cmul(a8r, a8i, sr, si)
        return er, ei, cr + xr[last:last + 1], ci + xi[last:last + 1]

    def sweep(n, base, carry):
        nblk = n // rpb

        def body(i, carry):
            sfr, sfi, sbr, sbi = carry
            rf = pl.multiple_of(base + i * rpb, rpb)
            rb = pl.multiple_of(base + (nblk - 1 - i) * rpb, rpb)
            efr, efi, sfr, sfi = block_scan(z_ref[pl.ds(rf, rpb), 0 * sw:1 * sw],
                                            z_ref[pl.ds(rf, rpb), 1 * sw:2 * sw], sfr, sfi, 0, True)
            ebr, ebi, sbr, sbi = block_scan(z_ref[pl.ds(rb, rpb), 2 * sw:3 * sw],
                                            z_ref[pl.ds(rb, rpb), 3 * sw:4 * sw], sbr, sbi,
                                            2 * rpb, False)
            z_ref[pl.ds(rf, rpb), 0 * sw:1 * sw] = efr
            z_ref[pl.ds(rf, rpb), 1 * sw:2 * sw] = efi
            z_ref[pl.ds(rb, rpb), 2 * sw:3 * sw] = ebr
            z_ref[pl.ds(rb, rpb), 3 * sw:4 * sw] = ebi
            return sfr, sfi, sbr, sbi

        return lax.fori_loop(0, nblk, body, carry)

    zero = jnp.zeros((1, sw), jnp.float32)
    carry = sweep(jc, jl, (zero, zero, zero, zero))
    sweep(jl, 0, carry)

    y = jnp.dot(u_ref[...], wt_ref[...], preferred_element_type=jnp.float32)
    y = y + jnp.dot(z_ref[...].astype(jnp.bfloat16), wo_ref[...], preferred_element_type=jnp.float32)
    y = jax.nn.gelu(y)
    for t in range(L):
        yl_ref[pl.ds(t, jl, stride=L), :] = y[0:jl, t * lanes:(t + 1) * lanes].astype(yl_ref.dtype)
        yc_ref[pl.ds(t, jc, stride=L), :] = y[jl:jl + jc, t * lanes:(t + 1) * lanes].astype(yc_ref.dtype)


def _s5_mixer(lay, h, params, out_dtype):
    t_cat, win_cat, wout_cat, a8 = params
    b, d = lay.batch, lay.d
    L = S5_CHUNK
    jl, jc = lay.seq // L, lay.ctx // L
    assert jl % SUBLANES == 0 and jc % SUBLANES == 0 and (b * jl) % jc == 0
    cw = L * LANES
    sw4 = win_cat.shape[2]
    rows = jl + jc
    ctx0 = lay.tl // lay.ctx
    once = pl.Buffered(1)
    vmem = 2 * _nbytes((rows, L, LANES), h.dtype) + 2 * _nbytes((rows, L, LANES), out_dtype) \
        + 2 * _nbytes((cw, cw), jnp.bfloat16) + _nbytes((cw, sw4), jnp.bfloat16) \
        + _nbytes((rows, cw), jnp.bfloat16) + 4 * _nbytes((rows, max(cw, sw4)), jnp.float32)
    yl, yc = pl.pallas_call(
        functools.partial(_s5_kernel, jl=jl, jc=jc),
        grid=(d // LANES, b),
        in_specs=[
            pl.BlockSpec((lay.seq, LANES), lambda cb, bi: (bi, cb)),
            pl.BlockSpec((lay.ctx, LANES), lambda cb, bi: (ctx0 + bi, cb)),
            pl.BlockSpec((None, cw, sw4), lambda cb, bi: (cb, 0, 0), pipeline_mode=once),
            pl.BlockSpec((None, cw, cw), lambda cb, bi: (cb, 0, 0), pipeline_mode=once),
            pl.BlockSpec((None, sw4, cw), lambda cb, bi: (cb, 0, 0), pipeline_mode=once),
            pl.BlockSpec((None, a8.shape[1], a8.shape[2]), lambda cb, bi: (cb, 0, 0)),
        ],
        out_specs=[pl.BlockSpec((lay.seq, LANES), lambda cb, bi: (bi, cb)),
                   pl.BlockSpec((lay.ctx, LANES), lambda cb, bi: (bi, cb))],
        out_shape=[jax.ShapeDtypeStruct((lay.tl, d), out_dtype),
                   jax.ShapeDtypeStruct((lay.tc, d), out_dtype)],
        scratch_shapes=[pltpu.VMEM((rows, cw), jnp.bfloat16), pltpu.VMEM((rows, sw4), jnp.float32)],
        compiler_params=_cparams(("arbitrary", "arbitrary"), vmem),
        name="s5_chunked",
    )(h, h, win_cat, t_cat, wout_cat, a8)
    return jnp.concatenate([yl, yc], axis=0).astype(jnp.bfloat16)


def _rope_tables(lay, bm):
    n = lay.seq
    rows = n // GRID_W
    row = jnp.broadcast_to(jnp.arange(rows)[:, None], (rows, GRID_W)).reshape(-1)
    col = jnp.broadcast_to(jnp.arange(GRID_W)[None, :], (rows, GRID_W)).reshape(-1)
    n_freq = HEAD_DIM // 4
    inv_freq = ROPE_THETA ** (-jnp.arange(n_freq, dtype=jnp.float32) / n_freq)
    ang_r = row.astype(jnp.float32)[:, None] * inv_freq
    ang_c = col.astype(jnp.float32)[:, None] * inv_freq
    cr, sr, cc, sc = jnp.cos(ang_r), jnp.sin(ang_r), jnp.cos(ang_c), jnp.sin(ang_c)
    cos = jnp.concatenate([cr, cr, cc, cc], axis=-1)
    sin = jnp.concatenate([-sr, sr, -sc, sc], axis=-1)
    cos = jnp.concatenate([cos, jnp.ones((bm, HEAD_DIM), jnp.float32)], axis=0)
    sin = jnp.concatenate([sin, jnp.zeros((bm, HEAD_DIM), jnp.float32)], axis=0)
    return cos, sin


def _norm_rope(x, gain, cos, sin, scale):
    ms = jnp.mean(x * x, axis=-1, keepdims=True)
    xn = x * lax.rsqrt(ms + EPS) * gain
    q4 = HEAD_DIM // 4
    lane = lax.broadcasted_iota(jnp.int32, xn.shape, 1)
    partner = jnp.where(lane % (2 * q4) < q4,
                        pltpu.roll(xn, HEAD_DIM - q4, 1),
                        pltpu.roll(xn, q4, 1))
    out = xn * cos + partner * sin
    return out * scale if scale != 1.0 else out


def _q_epilogue(accs, extra_refs, out_refs, i, j, *, scale):
    gain_ref, cos_ref, sin_ref = extra_refs
    acc = accs[0]
    cos, sin, gain = cos_ref[...], sin_ref[...], gain_ref[...]
    for hh in range(acc.shape[1] // HEAD_DIM):
        sl = slice(hh * HEAD_DIM, (hh + 1) * HEAD_DIM)
        out_refs[0][:, sl] = _norm_rope(acc[:, sl], gain, cos, sin, scale).astype(out_refs[0].dtype)


def _kv_epilogue(accs, extra_refs, out_refs, i, j):
    gain_ref, cos_ref, sin_ref = extra_refs
    k_acc, v_acc = accs
    cos, sin, gain = cos_ref[...], sin_ref[...], gain_ref[...]
    for hh in range(k_acc.shape[1] // HEAD_DIM):
        sl = slice(hh * HEAD_DIM, (hh + 1) * HEAD_DIM)
        out_refs[0][:, sl] = _norm_rope(k_acc[:, sl], gain, cos, sin, 1.0).astype(out_refs[0].dtype)
    out_refs[1][...] = v_acc.astype(out_refs[1].dtype)


def _attn_kernel(q_ref, kx_ref, kc_ref, vx_ref, vc_ref, o_ref, *, with_latent):
    dn = (((1,), (1,)), ((), ()))
    for r in range(KV_REP):
        sl = slice(r * HEAD_DIM, (r + 1) * HEAD_DIM)
        q = q_ref[:, sl]
        sc = lax.dot_general(q, kc_ref[...], dn, preferred_element_type=jnp.float32)
        m = jnp.max(sc, axis=-1, keepdims=True)
        if with_latent:
            sx = lax.dot_general(q, kx_ref[...], dn, preferred_element_type=jnp.float32)
            m = jnp.maximum(m, jnp.max(sx, axis=-1, keepdims=True))
            px = jnp.exp(sx - m)
        pc = jnp.exp(sc - m)
        l = jnp.sum(pc, axis=-1, keepdims=True)
        o = jnp.dot(pc.astype(vc_ref.dtype), vc_ref[...], preferred_element_type=jnp.float32)
        if with_latent:
            l = l + jnp.sum(px, axis=-1, keepdims=True)
            o = o + jnp.dot(px.astype(vx_ref.dtype), vx_ref[...], preferred_element_type=jnp.float32)
        o_ref[:, sl] = (o / l).astype(o_ref.dtype)


def _attention(lay, q, k, v):
    t, dq = q.shape
    nkv = k.shape[1] // HEAD_DIM
    b, seq, ctx = lay.batch, lay.seq, lay.ctx
    qw = KV_REP * HEAD_DIM
    tq = _pick(seq, 256, SUBLANES)
    nq = seq // tq
    cblk0 = lay.tl // ctx
    kx_spec = pl.BlockSpec((seq, HEAD_DIM), lambda bi, g, qi: (bi, g))
    kc_spec = pl.BlockSpec((ctx, HEAD_DIM), lambda bi, g, qi: (cblk0 + bi, g))
    vmem = 4 * _nbytes((tq, qw), jnp.bfloat16) + 8 * _nbytes((seq + ctx, HEAD_DIM), jnp.bfloat16) \
        + 4 * _nbytes((tq, seq + ctx), jnp.float32)
    o_lat = pl.pallas_call(
        functools.partial(_attn_kernel, with_latent=True),
        grid=(b, nkv, nq),
        in_specs=[pl.BlockSpec((tq, qw), lambda bi, g, qi: (bi * nq + qi, g)),
                  kx_spec, kc_spec, kx_spec, kc_spec],
        out_specs=pl.BlockSpec((tq, qw), lambda bi, g, qi: (bi * nq + qi, g)),
        out_shape=jax.ShapeDtypeStruct((t, dq), jnp.bfloat16),
        compiler_params=_cparams(("parallel", "parallel", "arbitrary"), vmem),
        name="attn_latent",
    )(q, k, k, v, v)
    kc2 = pl.BlockSpec((ctx, HEAD_DIM), lambda bi, g: (cblk0 + bi, g))

    def ctx_kernel(q_ref, kc_ref, vc_ref, prev_ref, o_ref):
        del prev_ref
        _attn_kernel(q_ref, None, kc_ref, None, vc_ref, o_ref, with_latent=False)

    return pl.pallas_call(
        ctx_kernel,
        grid=(b, nkv),
        in_specs=[pl.BlockSpec((ctx, qw), lambda bi, g: (cblk0 + bi, g)), kc2, kc2,
                  pl.BlockSpec(memory_space=pl.ANY)],
        out_specs=pl.BlockSpec((ctx, qw), lambda bi, g: (cblk0 + bi, g)),
        out_shape=jax.ShapeDtypeStruct((t, dq), jnp.bfloat16),
        input_output_aliases={3: 0},
        compiler_params=_cparams(("parallel", "parallel"), 16 * 1024 * 1024),
        name="attn_context",
    )(q, k, v, o_lat)


def _attention_mixer(lay, h, w_qkv, layer, dq, dkv, q_gain, k_gain):
    t, d = h.shape
    bm = lay.bm
    cos, sin = _rope_tables(lay, bm)
    nseq = lay.seq // bm

    def pos_block(i):
        return jnp.where(i * bm < lay.tl, i % nseq, nseq)

    tab = lambda arr: (arr, (bm, HEAD_DIM), lambda i, j: (pos_block(i), 0))
    gain = lambda g: (g.reshape(1, HEAD_DIM), (1, HEAD_DIM), lambda i, j: (0, 0))
    bn = _pick(dq, 512)
    q = _matmul_wres("attn_q", h, w_qkv, layer, [0], [gain(q_gain), tab(cos), tab(sin)],
                     [_tile(jax.ShapeDtypeStruct((t, dq), jnp.bfloat16), bm, bn)],
                     functools.partial(_q_epilogue, scale=HEAD_DIM ** -0.5),
                     bm=bm, bn=bn, nn=dq // bn, epilogue_bytes=4 * _nbytes((bm, bn), jnp.float32))[0]
    bn = _pick(dkv, 256)
    o_sds = jax.ShapeDtypeStruct((t, dkv), jnp.bfloat16)
    k, v = _matmul_wres("attn_kv", h, w_qkv, layer, [dq // bn, (dq + dkv) // bn],
                        [gain(k_gain), tab(cos), tab(sin)],
                        [_tile(o_sds, bm, bn), _tile(o_sds, bm, bn)],
                        _kv_epilogue, bm=bm, bn=bn, nn=dkv // bn,
                        epilogue_bytes=4 * _nbytes((bm, bn), jnp.float32))
    return _attention(lay, q, k, v)


def _conv_in_epilogue(accs, extra_refs, out_refs, i, j):
    b_gate, c_gate, v = accs
    out_refs[0][...] = b_gate.astype(out_refs[0].dtype)
    out_refs[1][...] = (c_gate * v).astype(out_refs[1].dtype)


def _conv_kernel(b_ref, u_ref, up_ref, un_ref, w_ref, o_ref, *, rb, lay):
    i = pl.program_id(0)
    row0 = i * rb
    is_lat = row0 < lay.tl
    seq_len = jnp.where(is_lat, lay.seq, lay.ctx)
    local = jnp.where(is_lat, row0, row0 - lay.tl) % seq_len
    first = local == 0
    last = local + rb == seq_len
    u = u_ref[...].astype(jnp.float32)
    prev_row = jnp.where(first, 0.0, up_ref[SUBLANES - 1:SUBLANES, :].astype(jnp.float32))
    next_row = jnp.where(last, 0.0, un_ref[0:1, :].astype(jnp.float32))
    ridx = lax.broadcasted_iota(jnp.int32, u.shape, 0)
    u_prev = jnp.where(ridx == 0, prev_row, pltpu.roll(u, 1, 0))
    u_next = jnp.where(ridx == rb - 1, next_row, pltpu.roll(u, rb - 1, 0))
    w = w_ref[...]
    conv = w[0:1, :] * u_prev + w[1:2, :] * u + w[2:3, :] * u_next
    o_ref[...] = (b_ref[...].astype(jnp.float32) * conv).astype(o_ref.dtype)


def _short_conv(lay, b_gate, u, conv_w):
    t, d = u.shape
    rb = _pick(lay.ctx, 256, SUBLANES)
    assert lay.ctx % rb == 0 and lay.seq % rb == 0
    bc = _pick(d, 1024)
    sub = rb // SUBLANES
    nsub = t // SUBLANES
    w8 = jnp.concatenate([conv_w.astype(jnp.float32),
                          jnp.zeros((SUBLANES - conv_w.shape[0], d), jnp.float32)], axis=0)
    return pl.pallas_call(
        functools.partial(_conv_kernel, rb=rb, lay=lay),
        grid=(t // rb, d // bc),
        in_specs=[
            pl.BlockSpec((rb, bc), lambda i, j: (i, j)),
            pl.BlockSpec((rb, bc), lambda i, j: (i, j)),
            pl.BlockSpec((SUBLANES, bc), lambda i, j: (jnp.maximum(i * sub - 1, 0), j)),
            pl.BlockSpec((SUBLANES, bc), lambda i, j: (jnp.minimum((i + 1) * sub, nsub - 1), j)),
            pl.BlockSpec((SUBLANES, bc), lambda i, j: (0, j)),
        ],
        out_specs=pl.BlockSpec((rb, bc), lambda i, j: (i, j)),
        out_shape=jax.ShapeDtypeStruct((t, d), jnp.bfloat16),
        compiler_params=_cparams(("parallel", "parallel"),
                                 6 * _nbytes((rb, bc), jnp.bfloat16) + 8 * _nbytes((rb, bc), jnp.float32)),
        name="short_conv",
    )(b_gate, u, u, u, w8)


def _conv_mixer(lay, h, w_in, layer, conv_w):
    t, d = h.shape
    bm = lay.bm
    bn = _pick(d, 256)
    nn = d // bn
    o_sds = jax.ShapeDtypeStruct((t, d), jnp.bfloat16)
    b_gate, u = _matmul_wres("conv_in", h, w_in, layer, [0, nn, 2 * nn], [],
                             [_tile(o_sds, bm, bn), _tile(o_sds, bm, bn)],
                             _conv_in_epilogue, bm=bm, bn=bn, nn=nn, w_buffers=1,
                             epilogue_bytes=_nbytes((bm, bn), jnp.float32))
    return _short_conv(lay, b_gate, u, conv_w)


def kernel(x, c, ctx, c_ctx, ada_down, ada_up, ada_bias, norm1_g, norm2_g, s5_a_re, s5_a_im, s5_log_dt, s5_b_re, s5_b_im, s5_c_re, s5_c_im, s5_d, s5_w_glu, attn_w_qkv, attn_q_gain, attn_k_gain, attn_w_o, conv_w_in, conv_w, conv_w_out, mlp_w1, mlp_w2):
    batch, seq, d = x.shape
    n_ctx = ctx.shape[1]
    depth = ada_down.shape[0]
    lay = _Layout(batch, seq, n_ctx, d)
    bf16 = jnp.bfloat16

    xs = jnp.concatenate([x.reshape(lay.tl, d), ctx.reshape(lay.tc, d)], axis=0)

    cond = jnp.concatenate([c, c_ctx[None, :],
                            jnp.zeros((MOD_ROWS - batch - 1, d), c.dtype)], axis=0)
    mods = _ada_all(cond, ada_down, ada_up, ada_bias)
    mods = mods.reshape(depth, MOD_ROWS, N_MOD, 1, d).transpose(0, 2, 1, 3, 4)

    n_mixers = 3
    for i in range(depth):
        kind, j = i % n_mixers, i // n_mixers
        mod = mods[i]
        x_res = xs[:lay.tl] if i == depth - 1 else xs
        h = _norm_mod(lay, xs, norm1_g[i], mod, 0, 1,
                      out_dtype=jnp.float32 if kind == 0 else bf16)
        if kind == 0:
            params = _s5_prepare(s5_a_re[j], s5_a_im[j], s5_log_dt[j], s5_b_re[j], s5_b_im[j],
                                 s5_c_re[j], s5_c_im[j], s5_d[j])
            y = _s5_mixer(lay, h, params, jnp.float32)
            xs = _gated_residual_wres("s5_glu", lay, y, s5_w_glu, j, x_res, mod, 2, glu=True)
        elif kind == 1:
            dq = attn_w_o.shape[1]
            dkv = (attn_w_qkv.shape[2] - dq) // 2
            o = _attention_mixer(lay, h, attn_w_qkv, j, dq, dkv, attn_q_gain[j], attn_k_gain[j])
            xs = _gated_residual_wres("attn_out", lay, o, attn_w_o, j, x_res, mod, 2)
        else:
            bu = _conv_mixer(lay, h, conv_w_in, j, conv_w[j])
            xs = _gated_residual_wres("conv_out", lay, bu, conv_w_out, j, x_res, mod, 2)
        h = _norm_mod(lay, xs, norm2_g[i], mod, 3, 4)
        xs = _mlp(lay, h, mlp_w1, mlp_w2, i, xs, mod)
    return xs.reshape(batch, seq, d)
```

```python
import functools

import jax
import jax.numpy as jnp
from jax import lax
from jax.experimental import pallas as pl
from jax.experimental.pallas import tpu as pltpu

EPS = 1e-6
N_MOD = 6
S5_GROUP = 16
S5_CHUNK = 16
HEAD_DIM = 128
KV_REP = 4
GRID_W = 64
ROPE_THETA = 10000.0
LANES = 128
SUBLANES = 8
V7X_VMEM_BYTES = 64 * 1024 * 1024
VMEM_CEILING = V7X_VMEM_BYTES - 4 * 1024 * 1024
MOD_ROWS = 8

_HP = lax.Precision.HIGHEST


INTERNAL_SCRATCH_BYTES = 2 * 1024 * 1024


def _cparams(sem, vmem_bytes):
    vmem_bytes += INTERNAL_SCRATCH_BYTES
    return pltpu.CompilerParams(
        dimension_semantics=sem,
        vmem_limit_bytes=int(min(max(vmem_bytes, 16 * 1024 * 1024), VMEM_CEILING)))


def _nbytes(shape, dtype):
    n = 1
    for s in shape:
        if s is not None:
            n *= s
    return n * jnp.dtype(dtype).itemsize


def _pick(n, target, mult=LANES):
    best = None
    for d in range(mult, min(n, target) + 1, mult):
        if n % d == 0:
            best = d
    return best if best is not None else n


def _ada_kernel(cond_ref, down_ref, up_ref, b_ref, o_ref, t_ref):
    @pl.when(pl.program_id(1) == 0)
    def _():
        cnd = cond_ref[...]
        s = cnd * jax.nn.sigmoid(cnd)
        t_ref[...] = jnp.dot(s.astype(jnp.bfloat16), down_ref[...].astype(jnp.bfloat16),
                             preferred_element_type=jnp.float32)

    o_ref[...] = jnp.dot(t_ref[...].astype(jnp.bfloat16), up_ref[...].astype(jnp.bfloat16),
                         preferred_element_type=jnp.float32) + b_ref[...]


def _ada_all(cond, ada_down, ada_up, ada_bias):
    depth, d, r = ada_down.shape
    n = ada_up.shape[2]
    bn = _pick(n, 2048)
    vmem = 2 * (_nbytes((MOD_ROWS, d), jnp.float32) + _nbytes((d, r), jnp.float32)
                + _nbytes((r, bn), jnp.float32) + 2 * _nbytes((MOD_ROWS, bn), jnp.float32))
    return pl.pallas_call(
        _ada_kernel,
        grid=(depth, n // bn),
        in_specs=[
            pl.BlockSpec((MOD_ROWS, d), lambda l, j: (0, 0)),
            pl.BlockSpec((None, d, r), lambda l, j: (l, 0, 0)),
            pl.BlockSpec((None, r, bn), lambda l, j: (l, 0, j)),
            pl.BlockSpec((None, 1, bn), lambda l, j: (l, 0, j)),
        ],
        out_specs=pl.BlockSpec((None, MOD_ROWS, bn), lambda l, j: (l, 0, j)),
        out_shape=jax.ShapeDtypeStruct((depth, MOD_ROWS, n), jnp.float32),
        scratch_shapes=[pltpu.VMEM((MOD_ROWS, r), jnp.float32)],
        compiler_params=_cparams(("arbitrary", "arbitrary"), 2 * vmem),
        name="ada_mod",
    )(cond, ada_down, ada_up, ada_bias.reshape(depth, 1, n))


def _norm_kernel(x_ref, g_ref, sh_ref, sc_ref, o_ref):
    x = x_ref[...]
    ms = jnp.mean(x * x, axis=-1, keepdims=True)
    y = x * lax.rsqrt(ms + EPS) * g_ref[...]
    o_ref[...] = (y * (1.0 + sc_ref[...]) + sh_ref[...]).astype(o_ref.dtype)


class _Layout:
    def __init__(self, batch, seq, ctx, d):
        self.batch, self.seq, self.ctx, self.d = batch, seq, ctx, d
        self.tl, self.tc = batch * seq, batch * ctx
        self.t = self.tl + self.tc
        self.bm = _pick(self.tc, 1024, SUBLANES)
        assert seq % self.bm == 0 and self.tc % self.bm == 0
        assert batch + 1 <= MOD_ROWS

    def mod_row(self, i, bm):
        return jnp.where(i * bm < self.tl, (i * bm) // self.seq, self.batch)


def _norm_mod(lay, x, g, mod, shift_idx, scale_idx, out_dtype=jnp.bfloat16):
    t, d = x.shape
    bm = _pick(lay.bm, 512, SUBLANES)
    vmem = 2 * (_nbytes((bm, d), jnp.float32) + _nbytes((bm, d), out_dtype)) \
        + 4 * _nbytes((bm, d), jnp.float32)

    def mod_spec(which):
        return pl.BlockSpec((None, None, 1, d), lambda i: (which, lay.mod_row(i, bm), 0, 0))

    return pl.pallas_call(
        _norm_kernel,
        grid=(t // bm,),
        in_specs=[
            pl.BlockSpec((bm, d), lambda i: (i, 0)),
            pl.BlockSpec((1, d), lambda i: (0, 0)),
            mod_spec(shift_idx),
            mod_spec(scale_idx),
        ],
        out_specs=pl.BlockSpec((bm, d), lambda i: (i, 0)),
        out_shape=jax.ShapeDtypeStruct((t, d), out_dtype),
        compiler_params=_cparams(("parallel",), vmem),
        name="norm_mod",
    )(x, g.reshape(1, d), mod, mod)


def _tile(arr, bm, bn):
    return (arr, (bm, bn), lambda i, j: (i, j))


def _mod_op(lay, mod, which, bm, bn):
    return (mod, (None, None, 1, bn), lambda i, j: (which, lay.mod_row(i, bm), 0, j))


def _mm_wres_kernel(*refs, n_rhs, n_extra, n_out, epilogue):
    lhs_ref = refs[0]
    w_refs = refs[1:1 + n_rhs]
    extra_refs = refs[1 + n_rhs:1 + n_rhs + n_extra]
    out_refs = refs[1 + n_rhs + n_extra:1 + n_rhs + n_extra + n_out]
    wbf_refs = refs[1 + n_rhs + n_extra + n_out:]
    j, i = pl.program_id(0), pl.program_id(1)

    @pl.when(i == 0)
    def _():
        for wb, w in zip(wbf_refs, w_refs):
            wb[...] = w[...].astype(wb.dtype)

    accs = [jnp.dot(lhs_ref[...], wb[...], preferred_element_type=jnp.float32) for wb in wbf_refs]
    epilogue(accs, extra_refs, out_refs, i, j)


def _matmul_wres(name, lhs, w, layer, offs, extras, outs, epilogue, *, bm, bn, nn,
                 kblk=None, kidx=0, w_buffers=2, epilogue_bytes=0):
    m = outs[0][0].shape[0]
    kdim = lhs.shape[1] if kblk is None else kblk
    assert m % bm == 0 and m <= lhs.shape[0]
    assert w.shape[1] == lhs.shape[1] and lhs.shape[1] % kdim == 0
    w_mode = {"pipeline_mode": pl.Buffered(1)} if w_buffers == 1 else {}

    def bind(shape, fn):
        return pl.BlockSpec(shape, lambda j, i: fn(i, j))

    in_specs = [pl.BlockSpec((bm, kdim), lambda j, i: (i, kidx))]
    for off in offs:
        in_specs.append(pl.BlockSpec((None, kdim, bn), lambda j, i, off=off: (layer, kidx, off + j),
                                     **w_mode))
    in_specs += [bind(s, f) for _, s, f in extras]
    vmem = 2 * _nbytes((bm, kdim), lhs.dtype)
    vmem += len(offs) * (w_buffers * _nbytes((kdim, bn), w.dtype) + _nbytes((kdim, bn), jnp.bfloat16))
    vmem += sum(2 * _nbytes(s, a.dtype) for a, s, _ in extras + outs)
    vmem += 2 * len(offs) * _nbytes((bm, bn), jnp.float32) + epilogue_bytes
    kern = functools.partial(_mm_wres_kernel, n_rhs=len(offs), n_extra=len(extras),
                             n_out=len(outs), epilogue=epilogue)
    return pl.pallas_call(
        kern,
        grid=(nn, m // bm),
        in_specs=in_specs,
        out_specs=[bind(s, f) for _, s, f in outs],
        out_shape=[a for a, _, _ in outs],
        scratch_shapes=[pltpu.VMEM((kdim, bn), jnp.bfloat16) for _ in offs],
        compiler_params=_cparams(("arbitrary", "arbitrary"), vmem),
        name=name,
    )(lhs, *[w for _ in offs], *[a for a, _, _ in extras])


def _residual_epilogue(accs, extra_refs, out_refs, i, j):
    res_ref, gate_ref = extra_refs
    out_refs[0][...] = res_ref[...] + gate_ref[...] * accs[0]


def _glu_residual_epilogue(accs, extra_refs, out_refs, i, j):
    res_ref, gate_ref = extra_refs
    a, g = accs
    out_refs[0][...] = res_ref[...] + gate_ref[...] * (a * jax.nn.sigmoid(g))


def _relu2_epilogue(accs, extra_refs, out_refs, i, j):
    r = jnp.maximum(accs[0], 0.0)
    out_refs[0][...] = (r * r).astype(out_refs[0].dtype)


def _gated_residual_wres(name, lay, lhs, w, layer, x_res, mod, gate_idx, *, glu=False,
                         kblk=None, kidx=0):
    t, d = x_res.shape
    bm = lay.bm
    bn = _pick(d, 256 if glu else 512)
    nn = d // bn
    extras = [_tile(x_res, bm, bn), _mod_op(lay, mod, gate_idx, bm, bn)]
    outs = [_tile(jax.ShapeDtypeStruct((t, d), jnp.float32), bm, bn)]
    epi = _glu_residual_epilogue if glu else _residual_epilogue
    return _matmul_wres(name, lhs, w, layer, [0, nn] if glu else [0], extras, outs, epi,
                        bm=bm, bn=bn, nn=nn, kblk=kblk, kidx=kidx,
                        epilogue_bytes=2 * _nbytes((bm, bn), jnp.float32))[0]


def _mlp(lay, h, w1, w2, layer, x_res, mod):
    t, d = x_res.shape
    dff = w1.shape[2]
    bm = lay.bm
    bn = _pick(dff, 1024)
    a = _matmul_wres("mlp_up", h, w1, layer, [0], [],
                     [_tile(jax.ShapeDtypeStruct((t, dff), jnp.bfloat16), bm, bn)],
                     _relu2_epilogue, bm=bm, bn=bn, nn=dff // bn, w_buffers=1,
                     epilogue_bytes=_nbytes((bm, bn), jnp.float32))[0]
    kblk = _pick(dff, d)
    for c in range(dff // kblk):
        x_res = _gated_residual_wres("mlp_down", lay, a, w2, layer, x_res, mod, 5,
                                     kblk=kblk, kidx=c)
    return x_res


def _cmul(ar, ai, br, bi):
    return ar * br - ai * bi, ar * bi + ai * br


def _expand_groups(compact, outer, inner, gpb, row_group):
    k = outer * inner
    col = jnp.arange(outer * gpb * inner)
    src = (col // (gpb * inner)) * inner + col % inner
    spread = (jnp.arange(k)[:, None] == src[None, :]).astype(compact.dtype)
    keep = row_group[:, None] == ((col // inner) % gpb)[None, :]
    full = jnp.einsum('brk,kn->brn', compact, spread, preferred_element_type=jnp.float32)
    return jnp.where(keep[None], full, 0.0).astype(compact.dtype)


def _s5_prepare(a_re, a_im, log_dt, b_re, b_im, c_re, c_im, d_skip):
    f32 = jnp.float32
    L = S5_CHUNK
    a_re = jnp.minimum(a_re.astype(f32), -1e-4)
    a_im = a_im.astype(f32)
    dt = jnp.exp(log_dt.astype(f32))[..., None]
    mag = jnp.exp(a_re * dt)
    ab_re = mag * jnp.cos(a_im * dt)
    ab_im = mag * jnp.sin(a_im * dt)
    den = a_re * a_re + a_im * a_im
    n_re = ab_re - 1.0
    q_re = (n_re * a_re + ab_im * a_im) / den
    q_im = (ab_im * a_re - n_re * a_im) / den
    b_re, b_im = b_re.astype(f32), b_im.astype(f32)
    bb_re = q_re[..., None] * b_re - q_im[..., None] * b_im
    bb_im = q_re[..., None] * b_im + q_im[..., None] * b_re
    c_re, c_im = c_re.astype(f32), c_im.astype(f32)
    bb_re, bb_im, c_re, c_im = [v.astype(jnp.bfloat16).astype(f32) for v in (bb_re, bb_im, c_re, c_im)]
    ngroups, nstate = ab_re.shape[1], ab_re.shape[2]
    cg = b_re.shape[-1]
    gpb = LANES // cg
    ncb = ngroups // gpb
    bf16 = jnp.bfloat16

    pr, pi = [jnp.ones_like(ab_re)], [jnp.zeros_like(ab_im)]
    for _ in range(L):
        nr, ni = _cmul(pr[-1], pi[-1], ab_re, ab_im)
        pr.append(nr)
        pi.append(ni)
    pw_re, pw_im = jnp.stack(pr), jnp.stack(pi)

    cp_re, cp_im = _cmul(c_re[None], c_im[None], pw_re[:L, :, :, None, :], pw_im[:L, :, :, None, :])
    kern = (jnp.einsum('kdgcp,dgpe->kdgce', cp_re, bb_re, precision=_HP)
            - jnp.einsum('kdgcp,dgpe->kdgce', cp_im, bb_im, precision=_HP))
    kf, kb = kern[:, 0], kern[:, 1]
    diag = jnp.eye(cg, dtype=f32)[None] * d_skip.astype(f32).reshape(ngroups, cg)[:, :, None]
    k_all = jnp.concatenate([kb[:0:-1], (kf[0] + kb[0] + diag)[None], kf[1:]], axis=0)
    s_idx = jnp.arange(L)[:, None]
    t_idx = jnp.arange(L)[None, :]
    tm = k_all[t_idx - s_idx + (L - 1)]
    tm = tm.transpose(2, 0, 4, 1, 3).astype(bf16)
    token_rows = (jnp.arange(L * LANES) // cg) % gpb
    state_rows = (jnp.arange(4 * gpb * nstate) // nstate) % gpb
    tc = tm.reshape(ncb, gpb, L, cg, L * cg).transpose(0, 2, 1, 3, 4).reshape(ncb, L * LANES, L * cg)
    t_cat = _expand_groups(tc, L, cg, gpb, token_rows)

    def win_dir(d, pows_re, pows_im):
        wr, wi = _cmul(pows_re[:, :, None, :], pows_im[:, :, None, :],
                       jnp.swapaxes(bb_re[d], 1, 2)[None], jnp.swapaxes(bb_im[d], 1, 2)[None])
        return wr, wi

    def pows(ks, d):
        return jnp.stack([pr[k][d] for k in ks]), jnp.stack([pi[k][d] for k in ks])

    wf_re, wf_im = win_dir(0, *pows(range(L - 1, -1, -1), 0))
    wb_re, wb_im = win_dir(1, *pows(range(L), 1))
    w4 = jnp.stack([wf_re, wf_im, wb_re, wb_im], axis=3).astype(bf16)
    wc = w4.reshape(L, ncb, gpb, cg, 4 * nstate).transpose(1, 0, 2, 3, 4)
    win_cat = _expand_groups(wc.reshape(ncb, L * LANES, 4 * nstate), 4, nstate, gpb, token_rows)

    def wout_dir(d, pows_re, pows_im):
        vr, vi = _cmul(c_re[d][None], c_im[d][None], pows_re[:, :, None, :], pows_im[:, :, None, :])
        return vr, vi

    vf_re, vf_im = wout_dir(0, *pows(range(1, L + 1), 0))
    vb_re, vb_im = wout_dir(1, *pows(range(L, 0, -1), 1))
    v4 = jnp.stack([vf_re, -vf_im, vb_re, -vb_im], axis=0).astype(bf16)
    vc = v4.reshape(4, L, ncb, gpb, cg, nstate).transpose(2, 0, 3, 5, 1, 4)
    wout_cat = _expand_groups(vc.reshape(ncb, 4 * gpb * nstate, L * cg), L, cg, gpb, state_rows)

    qr, qi = [jnp.ones_like(ab_re)], [jnp.zeros_like(ab_im)]
    for _ in range(SUBLANES):
        nr, ni = _cmul(qr[-1], qi[-1], pw_re[L], pw_im[L])
        qr.append(nr)
        qi.append(ni)
    asc, desc = range(SUBLANES), range(SUBLANES - 1, -1, -1)
    rows = ([qr[k][0] for k in asc] + [qi[k][0] for k in asc]
            + [qr[k][1] for k in desc] + [qi[k][1] for k in desc])
    for d in (0, 1):
        for k in (1, 2, 4, 8):
            rows += [qr[k][d], qi[k][d]]
    a8 = jnp.stack(rows, axis=0)
    a8 = a8.reshape(len(rows), ncb, gpb * nstate).transpose(1, 0, 2)
    return t_cat, win_cat, wout_cat, a8


def _s5_kernel(hl_ref, hc_ref, wz_ref, wt_ref, wo_ref, a_ref, yl_ref, yc_ref, u_ref, z_ref,
               *, jl, jc):
    L, lanes = S5_CHUNK, hl_ref.shape[1]
    for t in range(L):
        u_ref[0:jl, t * lanes:(t + 1) * lanes] = hl_ref[pl.ds(t, jl, stride=L), :].astype(u_ref.dtype)
        u_ref[jl:jl + jc, t * lanes:(t + 1) * lanes] = hc_ref[pl.ds(t, jc, stride=L), :].astype(u_ref.dtype)
    z_ref[...] = jnp.dot(u_ref[...], wz_ref[...], preferred_element_type=jnp.float32)

    sw = z_ref.shape[1] // 4
    rpb = SUBLANES
    rowid = lax.broadcasted_iota(jnp.int32, (rpb, sw), 0)

    def shift(x, k, down):
        if down:
            return jnp.where(rowid >= k, pltpu.roll(x, k, 0), 0.0)
        return jnp.where(rowid < rpb - k, pltpu.roll(x, rpb - k, 0), 0.0)

    def block_scan(zr, zi, sr, si, tab, down):
        apr, api = a_ref[tab:tab + rpb, :], a_ref[tab + rpb:tab + 2 * rpb, :]
        pw = SUBLANES * 4 + (0 if down else SUBLANES)
        xr, xi = zr, zi
        for n, k in enumerate((1, 2, 4)):
            ar, ai = a_ref[pw + 2 * n:pw + 2 * n + 1, :], a_ref[pw + 2 * n + 1:pw + 2 * n + 2, :]
            mr, mi = _cmul(ar, ai, shift(xr, k, down), shift(xi, k, down))
            xr, xi = xr + mr, xi + mi
        er, ei = _cmul(apr, api, sr, si)
        er, ei = er + shift(xr, 1, down), ei + shift(xi, 1, down)
        a8r, a8i = a_ref[pw + 6:pw + 7, :], a_ref[pw + 7:pw + 8, :]
        last = rpb - 1 if down else 0
        cr, ci = _cmul(a8r, a8i, sr, si)
        return er, ei, cr + xr[last:last + 1], ci + xi[last:last + 1]

    def sweep(n, base, carry):
        nblk = n // rpb

        def body(i, carry):
            sfr, sfi, sbr, sbi = carry
            rf = pl.multiple_of(base + i * rpb, rpb)
            rb = pl.multiple_of(base + (nblk - 1 - i) * rpb, rpb)
            efr, efi, sfr, sfi = block_scan(z_ref[pl.ds(rf, rpb), 0 * sw:1 * sw],
                                            z_ref[pl.ds(rf, rpb), 1 * sw:2 * sw], sfr, sfi, 0, True)
            ebr, ebi, sbr, sbi = block_scan(z_ref[pl.ds(rb, rpb), 2 * sw:3 * sw],
                                            z_ref[pl.ds(rb, rpb), 3 * sw:4 * sw], sbr, sbi,
                                            2 * rpb, False)
            z_ref[pl.ds(rf, rpb), 0 * sw:1 * sw] = efr
            z_ref[pl.ds(rf, rpb), 1 * sw:2 * sw] = efi
            z_ref[pl.ds(rb, rpb), 2 * sw:3 * sw] = ebr
            z_ref[pl.ds(rb, rpb), 3 * sw:4 * sw] = ebi
            return sfr, sfi, sbr, sbi

        return lax.fori_loop(0, nblk, body, carry)

    zero = jnp.zeros((1, sw), jnp.float32)
    carry = sweep(jc, jl, (zero, zero, zero, zero))
    sweep(jl, 0, carry)

    y = jnp.dot(u_ref[...], wt_ref[...], preferred_element_type=jnp.float32)
    y = y + jnp.dot(z_ref[...].astype(jnp.bfloat16), wo_ref[...], preferred_element_type=jnp.float32)
    y = jax.nn.gelu(y)
    for t in range(L):
        yl_ref[pl.ds(t, jl, stride=L), :] = y[0:jl, t * lanes:(t + 1) * lanes].astype(yl_ref.dtype)
        yc_ref[pl.ds(t, jc, stride=L), :] = y[jl:jl + jc, t * lanes:(t + 1) * lanes].astype(yc_ref.dtype)


def _s5_mixer(lay, h, params, out_dtype):
    t_cat, win_cat, wout_cat, a8 = params
    b, d = lay.batch, lay.d
    L = S5_CHUNK
    jl, jc = lay.seq // L, lay.ctx // L
    assert jl % SUBLANES == 0 and jc % SUBLANES == 0 and (b * jl) % jc == 0
    cw = L * LANES
    sw4 = win_cat.shape[2]
    rows = jl + jc
    ctx0 = lay.tl // lay.ctx
    once = pl.Buffered(1)
    vmem = 2 * _nbytes((rows, L, LANES), h.dtype) + 2 * _nbytes((rows, L, LANES), out_dtype) \
        + 2 * _nbytes((cw, cw), jnp.bfloat16) + _nbytes((cw, sw4), jnp.bfloat16) \
        + _nbytes((rows, cw), jnp.bfloat16) + 4 * _nbytes((rows, max(cw, sw4)), jnp.float32)
    yl, yc = pl.pallas_call(
        functools.partial(_s5_kernel, jl=jl, jc=jc),
        grid=(d // LANES, b),
        in_specs=[
            pl.BlockSpec((lay.seq, LANES), lambda cb, bi: (bi, cb)),
            pl.BlockSpec((lay.ctx, LANES), lambda cb, bi: (ctx0 + bi, cb)),
            pl.BlockSpec((None, cw, sw4), lambda cb, bi: (cb, 0, 0), pipeline_mode=once),
            pl.BlockSpec((None, cw, cw), lambda cb, bi: (cb, 0, 0), pipeline_mode=once),
            pl.BlockSpec((None, sw4, cw), lambda cb, bi: (cb, 0, 0), pipeline_mode=once),
            pl.BlockSpec((None, a8.shape[1], a8.shape[2]), lambda cb, bi: (cb, 0, 0)),
        ],
        out_specs=[pl.BlockSpec((lay.seq, LANES), lambda cb, bi: (bi, cb)),
                   pl.BlockSpec((lay.ctx, LANES), lambda cb, bi: (bi, cb))],
        out_shape=[jax.ShapeDtypeStruct((lay.tl, d), out_dtype),
                   jax.ShapeDtypeStruct((lay.tc, d), out_dtype)],
        scratch_shapes=[pltpu.VMEM((rows, cw), jnp.bfloat16), pltpu.VMEM((rows, sw4), jnp.float32)],
        compiler_params=_cparams(("arbitrary", "arbitrary"), vmem),
        name="s5_chunked",
    )(h, h, win_cat, t_cat, wout_cat, a8)
    return jnp.concatenate([yl, yc], axis=0).astype(jnp.bfloat16)


def _rope_tables(lay, bm):
    n = lay.seq
    rows = n // GRID_W
    row = jnp.broadcast_to(jnp.arange(rows)[:, None], (rows, GRID_W)).reshape(-1)
    col = jnp.broadcast_to(jnp.arange(GRID_W)[None, :], (rows, GRID_W)).reshape(-1)
    n_freq = HEAD_DIM // 4
    inv_freq = ROPE_THETA ** (-jnp.arange(n_freq, dtype=jnp.float32) / n_freq)
    ang_r = row.astype(jnp.float32)[:, None] * inv_freq
    ang_c = col.astype(jnp.float32)[:, None] * inv_freq
    cr, sr, cc, sc = jnp.cos(ang_r), jnp.sin(ang_r), jnp.cos(ang_c), jnp.sin(ang_c)
    cos = jnp.concatenate([cr, cr, cc, cc], axis=-1)
    sin = jnp.concatenate([-sr, sr, -sc, sc], axis=-1)
    cos = jnp.concatenate([cos, jnp.ones((bm, HEAD_DIM), jnp.float32)], axis=0)
    sin = jnp.concatenate([sin, jnp.zeros((bm, HEAD_DIM), jnp.float32)], axis=0)
    return cos, sin


def _norm_rope(x, gain, cos, sin, scale):
    ms = jnp.mean(x * x, axis=-1, keepdims=True)
    xn = x * lax.rsqrt(ms + EPS) * gain
    q4 = HEAD_DIM // 4
    lane = lax.broadcasted_iota(jnp.int32, xn.shape, 1)
    partner = jnp.where(lane % (2 * q4) < q4,
                        pltpu.roll(xn, HEAD_DIM - q4, 1),
                        pltpu.roll(xn, q4, 1))
    out = xn * cos + partner * sin
    return out * scale if scale != 1.0 else out


def _q_epilogue(accs, extra_refs, out_refs, i, j, *, scale):
    gain_ref, cos_ref, sin_ref = extra_refs
    acc = accs[0]
    cos, sin, gain = cos_ref[...], sin_ref[...], gain_ref[...]
    for hh in range(acc.shape[1] // HEAD_DIM):
        sl = slice(hh * HEAD_DIM, (hh + 1) * HEAD_DIM)
        out_refs[0][:, sl] = _norm_rope(acc[:, sl], gain, cos, sin, scale).astype(out_refs[0].dtype)


def _kv_epilogue(accs, extra_refs, out_refs, i, j):
    gain_ref, cos_ref, sin_ref = extra_refs
    k_acc, v_acc = accs
    cos, sin, gain = cos_ref[...], sin_ref[...], gain_ref[...]
    for hh in range(k_acc.shape[1] // HEAD_DIM):
        sl = slice(hh * HEAD_DIM, (hh + 1) * HEAD_DIM)
        out_refs[0][:, sl] = _norm_rope(k_acc[:, sl], gain, cos, sin, 1.0).astype(out_refs[0].dtype)
    out_refs[1][...] = v_acc.astype(out_refs[1].dtype)


def _attn_kernel(q_ref, kx_ref, kc_ref, vx_ref, vc_ref, o_ref, *, with_latent):
    dn = (((1,), (1,)), ((), ()))
    for r in range(KV_REP):
        sl = slice(r * HEAD_DIM, (r + 1) * HEAD_DIM)
        q = q_ref[:, sl]
        sc = lax.dot_general(q, kc_ref[...], dn, preferred_element_type=jnp.float32)
        m = jnp.max(sc, axis=-1, keepdims=True)
        if with_latent:
            sx = lax.dot_general(q, kx_ref[...], dn, preferred_element_type=jnp.float32)
            m = jnp.maximum(m, jnp.max(sx, axis=-1, keepdims=True))
            px = jnp.exp(sx - m)
        pc = jnp.exp(sc - m)
        l = jnp.sum(pc, axis=-1, keepdims=True)
        o = jnp.dot(pc.astype(vc_ref.dtype), vc_ref[...], preferred_element_type=jnp.float32)
        if with_latent:
            l = l + jnp.sum(px, axis=-1, keepdims=True)
            o = o + jnp.dot(px.astype(vx_ref.dtype), vx_ref[...], preferred_element_type=jnp.float32)
        o_ref[:, sl] = (o / l).astype(o_ref.dtype)


def _attention(lay, q, k, v):
    t, dq = q.shape
    nkv = k.shape[1] // HEAD_DIM
    b, seq, ctx = lay.batch, lay.seq, lay.ctx
    qw = KV_REP * HEAD_DIM
    tq = _pick(seq, 256, SUBLANES)
    nq = seq // tq
    cblk0 = lay.tl // ctx
    kx_spec = pl.BlockSpec((seq, HEAD_DIM), lambda bi, g, qi: (bi, g))
    kc_spec = pl.BlockSpec((ctx, HEAD_DIM), lambda bi, g, qi: (cblk0 + bi, g))
    vmem = 4 * _nbytes((tq, qw), jnp.bfloat16) + 8 * _nbytes((seq + ctx, HEAD_DIM), jnp.bfloat16) \
        + 4 * _nbytes((tq, seq + ctx), jnp.float32)
    o_lat = pl.pallas_call(
        functools.partial(_attn_kernel, with_latent=True),
        grid=(b, nkv, nq),
        in_specs=[pl.BlockSpec((tq, qw), lambda bi, g, qi: (bi * nq + qi, g)),
                  kx_spec, kc_spec, kx_spec, kc_spec],
        out_specs=pl.BlockSpec((tq, qw), lambda bi, g, qi: (bi * nq + qi, g)),
        out_shape=jax.ShapeDtypeStruct((t, dq), jnp.bfloat16),
        compiler_params=_cparams(("parallel", "parallel", "arbitrary"), vmem),
        name="attn_latent",
    )(q, k, k, v, v)
    kc2 = pl.BlockSpec((ctx, HEAD_DIM), lambda bi, g: (cblk0 + bi, g))

    def ctx_kernel(q_ref, kc_ref, vc_ref, prev_ref, o_ref):
        del prev_ref
        _attn_kernel(q_ref, None, kc_ref, None, vc_ref, o_ref, with_latent=False)

    return pl.pallas_call(
        ctx_kernel,
        grid=(b, nkv),
        in_specs=[pl.BlockSpec((ctx, qw), lambda bi, g: (cblk0 + bi, g)), kc2, kc2,
                  pl.BlockSpec(memory_space=pl.ANY)],
        out_specs=pl.BlockSpec((ctx, qw), lambda bi, g: (cblk0 + bi, g)),
        out_shape=jax.ShapeDtypeStruct((t, dq), jnp.bfloat16),
        input_output_aliases={3: 0},
        compiler_params=_cparams(("parallel", "parallel"), 16 * 1024 * 1024),
        name="attn_context",
    )(q, k, v, o_lat)


def _attention_mixer(lay, h, w_qkv, layer, dq, dkv, q_gain, k_gain):
    t, d = h.shape
    bm = lay.bm
    cos, sin = _rope_tables(lay, bm)
    nseq = lay.seq // bm

    def pos_block(i):
        return jnp.where(i * bm < lay.tl, i % nseq, nseq)

    tab = lambda arr: (arr, (bm, HEAD_DIM), lambda i, j: (pos_block(i), 0))
    gain = lambda g: (g.reshape(1, HEAD_DIM), (1, HEAD_DIM), lambda i, j: (0, 0))
    bn = _pick(dq, 512)
    q = _matmul_wres("attn_q", h, w_qkv, layer, [0], [gain(q_gain), tab(cos), tab(sin)],
                     [_tile(jax.ShapeDtypeStruct((t, dq), jnp.bfloat16), bm, bn)],
                     functools.partial(_q_epilogue, scale=HEAD_DIM ** -0.5),
                     bm=bm, bn=bn, nn=dq // bn, epilogue_bytes=4 * _nbytes((bm, bn), jnp.float32))[0]
    bn = _pick(dkv, 256)
    o_sds = jax.ShapeDtypeStruct((t, dkv), jnp.bfloat16)
    k, v = _matmul_wres("attn_kv", h, w_qkv, layer, [dq // bn, (dq + dkv) // bn],
                        [gain(k_gain), tab(cos), tab(sin)],
                        [_tile(o_sds, bm, bn), _tile(o_sds, bm, bn)],
                        _kv_epilogue, bm=bm, bn=bn, nn=dkv // bn,
                        epilogue_bytes=4 * _nbytes((bm, bn), jnp.float32))
    return _attention(lay, q, k, v)


def _conv_in_epilogue(accs, extra_refs, out_refs, i, j):
    b_gate, c_gate, v = accs
    out_refs[0][...] = b_gate.astype(out_refs[0].dtype)
    out_refs[1][...] = (c_gate * v).astype(out_refs[1].dtype)


def _conv_kernel(b_ref, u_ref, up_ref, un_ref, w_ref, o_ref, *, rb, lay):
    i = pl.program_id(0)
    row0 = i * rb
    is_lat = row0 < lay.tl
    seq_len = jnp.where(is_lat, lay.seq, lay.ctx)
    local = jnp.where(is_lat, row0, row0 - lay.tl) % seq_len
    first = local == 0
    last = local + rb == seq_len
    u = u_ref[...].astype(jnp.float32)
    prev_row = jnp.where(first, 0.0, up_ref[SUBLANES - 1:SUBLANES, :].astype(jnp.float32))
    next_row = jnp.where(last, 0.0, un_ref[0:1, :].astype(jnp.float32))
    ridx = lax.broadcasted_iota(jnp.int32, u.shape, 0)
    u_prev = jnp.where(ridx == 0, prev_row, pltpu.roll(u, 1, 0))
    u_next = jnp.where(ridx == rb - 1, next_row, pltpu.roll(u, rb - 1, 0))
    w = w_ref[...]
    conv = w[0:1, :] * u_prev + w[1:2, :] * u + w[2:3, :] * u_next
    o_ref[...] = (b_ref[...].astype(jnp.float32) * conv).astype(o_ref.dtype)


def _short_conv(lay, b_gate, u, conv_w):
    t, d = u.shape
    rb = _pick(lay.ctx, 256, SUBLANES)
    assert lay.ctx % rb == 0 and lay.seq % rb == 0
    bc = _pick(d, 1024)
    sub = rb // SUBLANES
    nsub = t // SUBLANES
    w8 = jnp.concatenate([conv_w.astype(jnp.float32),
                          jnp.zeros((SUBLANES - conv_w.shape[0], d), jnp.float32)], axis=0)
    return pl.pallas_call(
        functools.partial(_conv_kernel, rb=rb, lay=lay),
        grid=(t // rb, d // bc),
        in_specs=[
            pl.BlockSpec((rb, bc), lambda i, j: (i, j)),
            pl.BlockSpec((rb, bc), lambda i, j: (i, j)),
            pl.BlockSpec((SUBLANES, bc), lambda i, j: (jnp.maximum(i * sub - 1, 0), j)),
            pl.BlockSpec((SUBLANES, bc), lambda i, j: (jnp.minimum((i + 1) * sub, nsub - 1), j)),
            pl.BlockSpec((SUBLANES, bc), lambda i, j: (0, j)),
        ],
        out_specs=pl.BlockSpec((rb, bc), lambda i, j: (i, j)),
        out_shape=jax.ShapeDtypeStruct((t, d), jnp.bfloat16),
        compiler_params=_cparams(("parallel", "parallel"),
                                 6 * _nbytes((rb, bc), jnp.bfloat16) + 8 * _nbytes((rb, bc), jnp.float32)),
        name="short_conv",
    )(b_gate, u, u, u, w8)


def _conv_mixer(lay, h, w_in, layer, conv_w):
    t, d = h.shape
    bm = lay.bm
    bn = _pick(d, 256)
    nn = d // bn
    o_sds = jax.ShapeDtypeStruct((t, d), jnp.bfloat16)
    b_gate, u = _matmul_wres("conv_in", h, w_in, layer, [0, nn, 2 * nn], [],
                             [_tile(o_sds, bm, bn), _tile(o_sds, bm, bn)],
                             _conv_in_epilogue, bm=bm, bn=bn, nn=nn, w_buffers=1,
                             epilogue_bytes=_nbytes((bm, bn), jnp.float32))
    return _short_conv(lay, b_gate, u, conv_w)


def kernel(x, c, ctx, c_ctx, ada_down, ada_up, ada_bias, norm1_g, norm2_g, s5_a_re, s5_a_im, s5_log_dt, s5_b_re, s5_b_im, s5_c_re, s5_c_im, s5_d, s5_w_glu, attn_w_qkv, attn_q_gain, attn_k_gain, attn_w_o, conv_w_in, conv_w, conv_w_out, mlp_w1, mlp_w2):
    batch, seq, d = x.shape
    n_ctx = ctx.shape[1]
    depth = ada_down.shape[0]
    lay = _Layout(batch, seq, n_ctx, d)
    bf16 = jnp.bfloat16

    xs = jnp.concatenate([x.reshape(lay.tl, d), ctx.reshape(lay.tc, d)], axis=0)

    cond = jnp.concatenate([c, c_ctx[None, :],
                            jnp.zeros((MOD_ROWS - batch - 1, d), c.dtype)], axis=0)
    mods = _ada_all(cond, ada_down, ada_up, ada_bias)
    mods = mods.reshape(depth, MOD_ROWS, N_MOD, 1, d).transpose(0, 2, 1, 3, 4)

    n_mixers = 3
    for i in range(depth):
        kind, j = i % n_mixers, i // n_mixers
        mod = mods[i]
        x_res = xs[:lay.tl] if i == depth - 1 else xs
        h = _norm_mod(lay, xs, norm1_g[i], mod, 0, 1,
                      out_dtype=jnp.float32 if kind == 0 else bf16)
        if kind == 0:
            params = _s5_prepare(s5_a_re[j], s5_a_im[j], s5_log_dt[j], s5_b_re[j], s5_b_im[j],
                                 s5_c_re[j], s5_c_im[j], s5_d[j])
            y = _s5_mixer(lay, h, params, jnp.float32)
            xs = _gated_residual_wres("s5_glu", lay, y, s5_w_glu, j, x_res, mod, 2, glu=True)
        elif kind == 1:
            dq = attn_w_o.shape[1]
            dkv = (attn_w_qkv.shape[2] - dq) // 2
            o = _attention_mixer(lay, h, attn_w_qkv, j, dq, dkv, attn_q_gain[j], attn_k_gain[j])
            xs = _gated_residual_wres("attn_out", lay, o, attn_w_o, j, x_res, mod, 2)
        else:
            bu = _conv_mixer(lay, h, conv_w_in, j, conv_w[j])
            xs = _gated_residual_wres("conv_out", lay, bu, conv_w_out, j, x_res, mod, 2)
        h = _norm_mod(lay, xs, norm2_g[i], mod, 3, 4)
        xs = _mlp(lay, h, mlp_w1, mlp_w2, i, xs, mod)
    return xs.reshape(batch, seq, d)
```
